```python
import jax, jax.numpy as jnp
from jax import lax
import numpy as np

D_MODEL = 1024
BATCH = 8
SEQ = 2048
DEPTH = 1
DEC_BATCH = 8
DEC_SEQ = 64
PAST_LEN = 4096

CHUNK = 64
Q_BLOCK = 128
FOX_HEADS = 8
FOX_DIM = 64
ML_HEADS = 4
ML_DIM = 128
ML_CONV = 4
MEM_HEADS = 4
MEM_DIM = 128
MEM_LEN = 256
D_FF = 2816
FFN_CONV = 3
N_BRANCH = 3
FORGET_BIAS = 3.0
EPS = 1e-6

FOX_W = FOX_HEADS * FOX_DIM
ML_W = ML_HEADS * ML_DIM
MEM_W = MEM_HEADS * MEM_DIM
IN_SIZES = (FOX_W, FOX_W, FOX_W, FOX_HEADS, ML_W, ML_W, ML_W, ML_HEADS, ML_HEADS, ML_W, MEM_W, N_BRANCH * D_MODEL)
IN_OFFSETS = tuple(sum(IN_SIZES[:i + 1]) for i in range(len(IN_SIZES)))
IN_W = IN_OFFSETS[-1]
SPLIT_IDX = list(IN_OFFSETS[:-1])
FOXF_LO, FOXF_HI = IN_OFFSETS[2], IN_OFFSETS[3]
MLF_LO, MLF_HI = IN_OFFSETS[7], IN_OFFSETS[8]

kernel_name = 'hybrid_fox_mlstm_streaming_encoder'


def rms_norm(x, g):
    xf = x.astype(jnp.float32)
    y = xf * lax.rsqrt(jnp.mean(xf * xf, axis=-1, keepdims=True) + EPS)
    return (y * g.astype(jnp.float32)).astype(x.dtype)


def causal_dwconv(x, prev, w, b):
    width = w.shape[0]
    t = x.shape[1]
    xp = jnp.concatenate([prev.astype(x.dtype), x], axis=1)
    y = sum(xp[:, j:j + t] * w[j] for j in range(width)) + b
    return y, xp[:, -(width - 1):]


def fox_attend(q, k, v, cq, ck, q_pos, k_pos):
    s = jnp.einsum('bqhd,bkhd->bhqk', q, k).astype(jnp.float32) * (FOX_DIM ** -0.5)
    s = s + jnp.transpose(cq, (0, 2, 1))[..., :, None] - jnp.transpose(ck, (0, 2, 1))[..., None, :]
    s = jnp.where(k_pos[None, :] <= q_pos[:, None], s, -jnp.inf)
    p = jax.nn.softmax(s, axis=-1)
    return jnp.einsum('bhqk,bkhd->bqhd', p.astype(v.dtype), v)


def mem_attend(q, k, v):
    s = jnp.einsum('bqhd,bmhd->bhqm', q, k).astype(jnp.float32) * (MEM_DIM ** -0.5)
    p = jax.nn.softmax(s, axis=-1)
    return jnp.einsum('bhqm,bmhd->bqhd', p.astype(v.dtype), v)


def memory_kv(mem, norm_mem, w_mem_kv):
    b, m, _ = mem.shape
    kv = rms_norm(mem, norm_mem) @ w_mem_kv
    k, v = jnp.split(kv, 2, axis=-1)
    return k.reshape(b, m, MEM_HEADS, MEM_DIM), v.reshape(b, m, MEM_HEADS, MEM_DIM)


def mlstm_chunkwise(q, k, v, i_pre, logf, c0, n0, m0):
    b_, t, h, d = q.shape
    L = min(CHUNK, t)
    n_blk = t // L

    def blocks(a):
        return a.reshape((b_, n_blk, L) + a.shape[2:])

    q, k, v = blocks(q), blocks(k), blocks(v)
    it = jnp.transpose(blocks(i_pre), (0, 1, 3, 2))
    bt = jnp.transpose(jnp.cumsum(blocks(logf), axis=2), (0, 1, 3, 2))
    b_end = bt[..., -1]
    g = b_end[..., None] - bt + it
    g_max = jnp.max(g, axis=-1)
    wg = jnp.exp(g - g_max[..., None])
    kv_blk = jnp.einsum('bnhs,bnshv,bnshk->bnhvk', wg, v, k)
    k_blk = jnp.einsum('bnhs,bnshk->bnhk', wg, k)

    def step(carry, xs):
        c, n, m = carry
        be, gm, kvb, kb = xs
        m_new = jnp.maximum(be + m, gm)
        decay = jnp.exp(be + m - m_new)
        scale = jnp.exp(gm - m_new)
        c_new = decay[..., None, None] * c + scale[..., None, None] * kvb
        n_new = decay[..., None] * n + scale[..., None] * kb
        return (c_new, n_new, m_new), (c, n, m)

    to_n = lambda a: jnp.moveaxis(a, 1, 0)
    (c_T, n_T, m_T), (c_p, n_p, m_p) = lax.scan(
        step, (c0, n0, m0), (to_n(b_end), to_n(g_max), to_n(kv_blk), to_n(k_blk)))
    c_p, n_p, m_p = jnp.moveaxis(c_p, 0, 1), jnp.moveaxis(n_p, 0, 1), jnp.moveaxis(m_p, 0, 1)

    log_w = bt[..., :, None] - bt[..., None, :] + it[..., None, :]
    causal = jnp.tril(jnp.ones((L, L), dtype=bool))
    log_w = jnp.where(causal, log_w, -jnp.inf)
    log_inter = bt + m_p[..., None]
    m_t = jnp.maximum(log_inter, jnp.max(log_w, axis=-1))
    s = jnp.einsum('bnthd,bnshd->bnhts', q, k) * jnp.exp(log_w - m_t[..., None])
    w_inter = jnp.exp(log_inter - m_t)
    num = (w_inter[..., None] * jnp.einsum('bnthk,bnhvk->bnhtv', q, c_p)
           + jnp.einsum('bnhts,bnshv->bnhtv', s, v))
    den = w_inter * jnp.einsum('bnthk,bnhk->bnht', q, n_p) + jnp.sum(s, axis=-1)
    hout = num / jnp.maximum(jnp.abs(den), jnp.exp(-m_t))[..., None]
    hout = jnp.transpose(hout, (0, 1, 3, 2, 4)).reshape(b_, t, h, d)
    return hout, c_T, n_T, m_T


def encoder_layer(x, fox_cache, ml_c0, ml_n0, ml_m0, ml_conv_prev, mem_k, mem_v, ffn_conv_prev, lw):
    (norm_mix_pre, w_in, b_in, fox_q_norm, fox_k_norm, mlstm_conv_w, mlstm_conv_b, mlstm_head_norm,
     w_br_a, w_br_b, w_br_m, w_out, norm_mix_post, norm_ffn_pre, w_up, ffn_conv_w, ffn_conv_b,
     w_down, norm_ffn_post) = lw
    f32 = jnp.float32
    B, T, _ = x.shape
    xn = rms_norm(x, norm_mix_pre)
    z = xn @ w_in + b_in
    fq, fk, fv, ff, mq, mk, mv, mi, mf, mo, cq, gate_logits = jnp.split(z, SPLIT_IDX, axis=-1)

    fq = rms_norm(fq.reshape(B, T, FOX_HEADS, FOX_DIM), fox_q_norm)
    fk = rms_norm(fk.reshape(B, T, FOX_HEADS, FOX_DIM), fox_k_norm)
    fv = fv.reshape(B, T, FOX_HEADS, FOX_DIM)
    f_log = jax.nn.log_sigmoid(ff.astype(f32))
    if fox_cache is None:
        c = jnp.cumsum(f_log, axis=1)
        pos = jnp.arange(T)
        outs = []
        for lo in range(0, T, Q_BLOCK):
            hi = min(lo + Q_BLOCK, T)
            outs.append(fox_attend(fq[:, lo:hi], fk[:, :hi], fv[:, :hi],
                                   c[:, lo:hi], c[:, :hi], pos[lo:hi], pos[:hi]))
        a_out = jnp.concatenate(outs, axis=1)
    else:
        k_cache, v_cache, logf_cache = fox_cache
        P = k_cache.shape[1]
        c_cache = jnp.cumsum(logf_cache.astype(f32), axis=1)
        c_cache = c_cache - c_cache[:, -1:]
        c_new = jnp.cumsum(f_log, axis=1)
        k_all = jnp.concatenate([k_cache.astype(fk.dtype), fk], axis=1)
        v_all = jnp.concatenate([v_cache.astype(fv.dtype), fv], axis=1)
        c_all = jnp.concatenate([c_cache, c_new], axis=1)
        a_out = fox_attend(fq, k_all, v_all, c_new, c_all, P + jnp.arange(T), jnp.arange(P + T))
    a_out = a_out.reshape(B, T, FOX_W)

    qk, ml_conv_new = causal_dwconv(jnp.concatenate([mq, mk], axis=-1), ml_conv_prev,
                                    mlstm_conv_w, mlstm_conv_b)
    qk = jax.nn.silu(qk)
    mq2, mk2 = jnp.split(qk, 2, axis=-1)
    hq = mq2.reshape(B, T, ML_HEADS, ML_DIM).astype(f32) * (ML_DIM ** -0.5)
    hk = mk2.reshape(B, T, ML_HEADS, ML_DIM).astype(f32)
    hv = mv.reshape(B, T, ML_HEADS, ML_DIM).astype(f32)
    h, c_T, n_T, m_T = mlstm_chunkwise(hq, hk, hv, mi.astype(f32), jax.nn.log_sigmoid(mf.astype(f32)),
                                       ml_c0.astype(f32), ml_n0.astype(f32), ml_m0.astype(f32))
    h = h * jax.nn.sigmoid(mo.astype(f32)).reshape(B, T, ML_HEADS, ML_DIM)
    b_out = rms_norm(h, mlstm_head_norm).astype(x.dtype).reshape(B, T, ML_W)

    m_out = mem_attend(cq.reshape(B, T, MEM_HEADS, MEM_DIM), mem_k.astype(x.dtype),
                       mem_v.astype(x.dtype)).reshape(B, T, MEM_W)

    gates = jax.nn.sigmoid(gate_logits.astype(f32)).astype(x.dtype).reshape(B, T, N_BRANCH, D_MODEL)
    merged = (gates[:, :, 0] * (a_out @ w_br_a) + gates[:, :, 1] * (b_out @ w_br_b)
              + gates[:, :, 2] * (m_out @ w_br_m))
    x = x + rms_norm(merged @ w_out, norm_mix_post)

    up = rms_norm(x, norm_ffn_pre) @ w_up
    up, ffn_conv_new = causal_dwconv(up, ffn_conv_prev, ffn_conv_w, ffn_conv_b)
    ua, ub = jnp.split(up, 2, axis=-1)
    hid = jax.nn.gelu(ua, approximate=True) * ub
    x = x + rms_norm(hid @ w_down, norm_ffn_post)
    return x, (fk, fv, f_log, c_T, n_T, m_T, ml_conv_new, ffn_conv_new)


def setup_inputs(seed: int = 0) -> dict:
    key = jax.random.key(seed)
    ks = list(jax.random.split(key, 48))

    def nrm(shape, scale=1.0):
        return scale * jax.random.normal(ks.pop(), shape, jnp.float32)

    def gain(shape):
        return 1.0 + 0.05 * nrm(shape)

    L = DEPTH
    b_in = nrm((L, IN_W), 0.02)
    b_in = b_in.at[:, FOXF_LO:FOXF_HI].add(FORGET_BIAS).at[:, MLF_LO:MLF_HI].add(FORGET_BIAS)
    return {
        'x_prompt': nrm((BATCH, SEQ, D_MODEL)),
        'x_sample': nrm((DEC_BATCH, DEC_SEQ, D_MODEL)),
        'cache_fox_k': nrm((L, DEC_BATCH, PAST_LEN, FOX_HEADS, FOX_DIM)),
        'cache_fox_v': nrm((L, DEC_BATCH, PAST_LEN, FOX_HEADS, FOX_DIM)),
        'cache_fox_logf': jax.nn.log_sigmoid(nrm((L, DEC_BATCH, PAST_LEN, FOX_HEADS)) + FORGET_BIAS),
        'state_mlstm_c': nrm((L, DEC_BATCH, ML_HEADS, ML_DIM, ML_DIM), 0.1),
        'state_mlstm_n': nrm((L, DEC_BATCH, ML_HEADS, ML_DIM), 0.1),
        'state_mlstm_m': nrm((L, DEC_BATCH, ML_HEADS), 0.5),
        'state_mlstm_conv': nrm((L, DEC_BATCH, ML_CONV - 1, 2 * ML_W)),
        'cache_mem_k': nrm((L, DEC_BATCH, MEM_LEN, MEM_HEADS, MEM_DIM)),
        'cache_mem_v': nrm((L, DEC_BATCH, MEM_LEN, MEM_HEADS, MEM_DIM)),
        'state_ffn_conv': nrm((L, DEC_BATCH, FFN_CONV - 1, 2 * D_FF)),
        'mem_prompt': nrm((BATCH, MEM_LEN, D_MODEL)),
        'norm_mix_pre': gain((L, D_MODEL)),
        'w_in': nrm((L, D_MODEL, IN_W), D_MODEL ** -0.5),
        'b_in': b_in,
        'fox_q_norm': gain((L, FOX_DIM)),
        'fox_k_norm': gain((L, FOX_DIM)),
        'mlstm_conv_w': nrm((L, ML_CONV, 2 * ML_W), ML_CONV ** -0.5),
        'mlstm_conv_b': nrm((L, 2 * ML_W), 0.02),
        'mlstm_head_norm': gain((L, ML_HEADS, ML_DIM)),
        'norm_mem': gain((L, D_MODEL)),
        'w_mem_kv': nrm((L, D_MODEL, 2 * MEM_W), D_MODEL ** -0.5),
        'w_br_a': nrm((L, FOX_W, D_MODEL), FOX_W ** -0.5),
        'w_br_b': nrm((L, ML_W, D_MODEL), ML_W ** -0.5),
        'w_br_m': nrm((L, MEM_W, D_MODEL), MEM_W ** -0.5),
        'w_out': nrm((L, D_MODEL, D_MODEL), D_MODEL ** -0.5),
        'norm_mix_post': gain((L, D_MODEL)),
        'norm_ffn_pre': gain((L, D_MODEL)),
        'w_up': nrm((L, D_MODEL, 2 * D_FF), D_MODEL ** -0.5),
        'ffn_conv_w': nrm((L, FFN_CONV, 2 * D_FF), FFN_CONV ** -0.5),
        'ffn_conv_b': nrm((L, 2 * D_FF), 0.02),
        'w_down': nrm((L, D_FF, D_MODEL), D_FF ** -0.5),
        'norm_ffn_post': gain((L, D_MODEL)),
    }


def reference(x_prompt, x_sample, cache_fox_k, cache_fox_v, cache_fox_logf, state_mlstm_c,
              state_mlstm_n, state_mlstm_m, state_mlstm_conv, cache_mem_k, cache_mem_v,
              state_ffn_conv, mem_prompt, norm_mix_pre, w_in, b_in, fox_q_norm, fox_k_norm,
              mlstm_conv_w, mlstm_conv_b, mlstm_head_norm, norm_mem, w_mem_kv, w_br_a, w_br_b,
              w_br_m, w_out, norm_mix_post, norm_ffn_pre, w_up, ffn_conv_w, ffn_conv_b, w_down,
              norm_ffn_post):
    f32 = jnp.float32
    B = x_prompt.shape[0]
    hp, hs = x_prompt, x_sample
    new_p = [[] for _ in range(10)]
    new_s = [[] for _ in range(8)]
    for l in range(DEPTH):
        lw = (norm_mix_pre[l], w_in[l], b_in[l], fox_q_norm[l], fox_k_norm[l], mlstm_conv_w[l],
              mlstm_conv_b[l], mlstm_head_norm[l], w_br_a[l], w_br_b[l], w_br_m[l], w_out[l],
              norm_mix_post[l], norm_ffn_pre[l], w_up[l], ffn_conv_w[l], ffn_conv_b[l], w_down[l],
              norm_ffn_post[l])
        mem_k_p, mem_v_p = memory_kv(mem_prompt, norm_mem[l], w_mem_kv[l])
        hp, st_p = encoder_layer(
            hp, None,
            jnp.zeros((B, ML_HEADS, ML_DIM, ML_DIM), f32), jnp.zeros((B, ML_HEADS, ML_DIM), f32),
            jnp.zeros((B, ML_HEADS), f32), jnp.zeros((B, ML_CONV - 1, 2 * ML_W), hp.dtype),
            mem_k_p, mem_v_p, jnp.zeros((B, FFN_CONV - 1, 2 * D_FF), hp.dtype), lw)
        hs, st_s = encoder_layer(
            hs, (cache_fox_k[l], cache_fox_v[l], cache_fox_logf[l]),
            state_mlstm_c[l], state_mlstm_n[l], state_mlstm_m[l], state_mlstm_conv[l],
            cache_mem_k[l], cache_mem_v[l], state_ffn_conv[l], lw)
        for acc, a in zip(new_p, st_p + (mem_k_p, mem_v_p)):
            acc.append(a)
        for acc, a in zip(new_s, st_s):
            acc.append(a)
    sp = [jnp.stack(a, axis=0) for a in new_p]
    ss = [jnp.stack(a, axis=0) for a in new_s]
    return (hp, hs, sp[0], sp[1], sp[2], sp[3], sp[4], sp[5], sp[6], sp[7], sp[8], sp[9],
            ss[0], ss[1], ss[2], ss[3], ss[4], ss[5], ss[6], ss[7])
```

```python
import functools

import jax
import jax.numpy as jnp
from jax import lax
from jax.experimental import pallas as pl
from jax.experimental.pallas import tpu as pltpu

F32 = jnp.float32
BF16 = jnp.bfloat16

D_MODEL = 1024
FOX_HEADS = 8
FOX_DIM = 64
ML_HEADS = 4
ML_DIM = 128
ML_CONV = 4
MEM_HEADS = 4
MEM_DIM = 128
D_FF = 2816
FFN_CONV = 3
N_BRANCH = 3
EPS = 1e-6
FOX_W = FOX_HEADS * FOX_DIM
ML_W = ML_HEADS * ML_DIM
MEM_W = MEM_HEADS * MEM_DIM

LANES = 128
SUBLANES = 8
NEG_BIG = -1e30
VMEM_LIMIT = 56 * 1024 * 1024

_C_FQ, _C_FK, _C_FV = 0, 512, 1024
_C_MQK, _C_MV, _C_MO, _C_CQ, _C_SM = 1536, 2560, 3072, 3584, 4096
_IN_COLS = 4224
_SM_FF, _SM_MI, _SM_MF = 0, 8, 12


def _params(*sem):
    return pltpu.CompilerParams(dimension_semantics=sem, vmem_limit_bytes=VMEM_LIMIT)


def _resident(shape):
    nd = len(shape)
    return pl.BlockSpec(shape, lambda *_: (0,) * nd, pipeline_mode=pl.Buffered(1))


def _rms(x, g):
    return x * lax.rsqrt(jnp.mean(x * x, axis=-1, keepdims=True) + EPS) * g


def _dot(a, b):
    return jnp.dot(a, b, preferred_element_type=F32)


def _dot_nt(a, b):
    return lax.dot_general(a, b, (((1,), (1,)), ((), ())), preferred_element_type=F32)


def _dot_tn(a, b):
    return lax.dot_general(a, b, (((0,), (0,)), ((), ())), preferred_element_type=F32)


def _log_sigmoid(z):
    return jnp.minimum(z, 0.0) - jnp.log1p(jnp.exp(-jnp.abs(z)))


def _sigmoid(z):
    return 1.0 / (1.0 + jnp.exp(-z))


def _in_proj_kernel(x_ref, g_ref, w_ref, b_ref, qn_ref, kn_ref, bd_ref,
                    q_ref, fk_ref, fv_ref, mqk_ref, mv_ref, mo_ref, cq_ref, sm_ref):
    xn = _rms(x_ref[...], g_ref[...]).astype(BF16)

    def proj(lo, hi):
        return _dot(xn, w_ref[:, lo:hi]) + b_ref[:, lo:hi]

    def head_norm(z, gn):
        ms = _dot((z * z).astype(BF16), bd_ref[...])
        return z * lax.rsqrt(ms + EPS) * gn

    q_ref[...] = (head_norm(proj(_C_FQ, _C_FK), qn_ref[...]) * (FOX_DIM ** -0.5)).astype(BF16)
    fk_ref[...] = head_norm(proj(_C_FK, _C_FV), kn_ref[...])
    fv_ref[...] = proj(_C_FV, _C_MQK)
    mqk_ref[...] = proj(_C_MQK, _C_MV)
    mv_ref[...] = proj(_C_MV, _C_MO).astype(BF16)
    mo_ref[...] = proj(_C_MO, _C_CQ)
    cq_ref[...] = (proj(_C_CQ, _C_SM) * (MEM_DIM ** -0.5)).astype(BF16)
    z = proj(_C_SM, _IN_COLS)
    lane = lax.broadcasted_iota(jnp.int32, z.shape, 1)
    is_forget = (lane < _SM_MI) | ((lane >= _SM_MF) & (lane < _SM_MF + ML_HEADS))
    sm_ref[...] = jnp.where(is_forget, _log_sigmoid(z), z)


def _in_proj(x2, W):
    n = x2.shape[0]
    tm = min(512, n)
    row = lambda w: pl.BlockSpec((tm, w), lambda i: (i, 0))
    outs = [(FOX_W, BF16), (FOX_W, F32), (FOX_W, F32), (2 * ML_W, F32), (ML_W, BF16),
            (ML_W, F32), (MEM_W, BF16), (LANES, F32)]
    return pl.pallas_call(
        _in_proj_kernel,
        grid=(n // tm,),
        in_specs=[row(D_MODEL), _resident((1, D_MODEL)), _resident((D_MODEL, _IN_COLS)),
                  _resident((1, _IN_COLS)), _resident((1, FOX_W)), _resident((1, FOX_W)),
                  _resident((FOX_W, FOX_W))],
        out_specs=[row(w) for w, _ in outs],
        out_shape=[jax.ShapeDtypeStruct((n, w), dt) for w, dt in outs],
        compiler_params=_params("parallel"),
        name="in_proj",
    )(x2, W["norm_mix_pre"], W["w_in"], W["b_in"], W["fox_q_norm"], W["fox_k_norm"], W["head_avg"])


def _norm_matmul_kernel(x_ref, g_ref, w_ref, o_ref):
    o_ref[...] = _dot(_rms(x_ref[...], g_ref[...]).astype(BF16), w_ref[...])


def _norm_matmul(x2, g, w):
    n, d = x2.shape
    cols = w.shape[1]
    tm = min(512, n)
    return pl.pallas_call(
        _norm_matmul_kernel,
        grid=(n // tm,),
        in_specs=[pl.BlockSpec((tm, d), lambda i: (i, 0)), _resident((1, d)), _resident((d, cols))],
        out_specs=pl.BlockSpec((tm, cols), lambda i: (i, 0)),
        out_shape=jax.ShapeDtypeStruct((n, cols), F32),
        compiler_params=_params("parallel"),
        name="mem_kv",
    )(x2, g, w)


def _cumsum_rows_kernel(f_ref, init_ref, o_ref, carry_ref):
    @pl.when(pl.program_id(1) == 0)
    def _():
        carry_ref[...] = init_ref[0]

    tb = f_ref.shape[-1]
    r = lax.broadcasted_iota(jnp.int32, (tb, tb), 0)
    c = lax.broadcasted_iota(jnp.int32, (tb, tb), 1)
    upper = (r <= c).astype(F32)
    cs = jnp.dot(f_ref[0], upper, precision=lax.Precision.HIGHEST,
                 preferred_element_type=F32) + carry_ref[:, 0:1]
    o_ref[0] = cs
    carry_ref[...] = jnp.broadcast_to(cs[:, tb - 1:tb], carry_ref.shape)


def _cumsum_rows(f_rows, init):
    b, r, t = f_rows.shape
    tb = min(256, t)
    return pl.pallas_call(
        _cumsum_rows_kernel,
        grid=(b, t // tb),
        in_specs=[pl.BlockSpec((1, r, tb), lambda i, j: (i, 0, j)),
                  pl.BlockSpec((1, r, LANES), lambda i, j: (i, 0, 0))],
        out_specs=pl.BlockSpec((1, r, tb), lambda i, j: (i, 0, j)),
        out_shape=jax.ShapeDtypeStruct((b, r, t), F32),
        scratch_shapes=[pltpu.VMEM((r, LANES), F32)],
        compiler_params=_params("parallel", "arbitrary"),
        name="cumsum_rows",
    )(f_rows, init)


def _fox_kernel(*refs, tq, n_cache_blocks, tkc, single_tile):
    if n_cache_blocks:
        (q_ref, kn_ref, vn_ref, cn_ref, kc_ref, vc_ref, cc_ref,
         o_ref, kb_ref, vb_ref, m_ref, l_ref, acc_ref) = refs
    else:
        q_ref, kn_ref, vn_ref, cn_ref, o_ref, kb_ref, vb_ref, m_ref, l_ref, acc_ref = refs
    qi = pl.program_id(2)

    @pl.when(qi == 0)
    def _():
        kb_ref[...] = kn_ref[0].astype(BF16)
        vb_ref[...] = vn_ref[0].astype(BF16)

    q = q_ref[0]
    lane = lax.broadcasted_iota(jnp.int32, q.shape, 1)
    zero = jnp.zeros_like(q)
    qh = (jnp.where(lane < FOX_DIM, q, zero), jnp.where(lane >= FOX_DIM, q, zero))
    t0 = pl.multiple_of(qi * tq, tq) if single_tile is False else 0
    c_anchor = [cn_ref[0, 0, hh:hh + 1, pl.ds(t0, tq)][:, 0:1] for hh in range(2)]

    m_ref[...] = jnp.full(m_ref.shape, NEG_BIG, F32)
    l_ref[...] = jnp.zeros(l_ref.shape, F32)
    acc_ref[...] = jnp.zeros(acc_ref.shape, F32)

    def block(hh, kblk, vblk, c_row, mask):
        s = _dot_nt(qh[hh], kblk) + (c_anchor[hh] - c_row)
        if mask is not None:
            s = jnp.where(mask, s, NEG_BIG)
        m_prev = m_ref[hh]
        m_new = jnp.maximum(m_prev, jnp.max(s, axis=-1, keepdims=True))
        alpha = jnp.exp(m_prev - m_new)
        p = jnp.exp(s - m_new)
        l_ref[hh] = alpha * l_ref[hh] + jnp.sum(p, axis=-1, keepdims=True)
        acc_ref[hh] = alpha * acc_ref[hh] + _dot(p.astype(BF16), vblk)
        m_ref[hh] = m_new

    if n_cache_blocks:
        def cache_body(j, carry):
            s0 = pl.multiple_of(j * tkc, tkc)
            kblk = kc_ref[0, pl.ds(s0, tkc), :].astype(BF16)
            vblk = vc_ref[0, pl.ds(s0, tkc), :].astype(BF16)
            for hh in range(2):
                block(hh, kblk, vblk, cc_ref[0, 0, hh:hh + 1, pl.ds(s0, tkc)], None)
            return carry
        lax.fori_loop(0, n_cache_blocks, cache_body, 0)

    def past_body(j, carry):
        s0 = pl.multiple_of(j * tq, tq)
        kblk = kb_ref[pl.ds(s0, tq), :]
        vblk = vb_ref[pl.ds(s0, tq), :]
        for hh in range(2):
            block(hh, kblk, vblk, cn_ref[0, 0, hh:hh + 1, pl.ds(s0, tq)], None)
        return carry
    if not single_tile:
        lax.fori_loop(0, qi, past_body, 0)

    rr = lax.broadcasted_iota(jnp.int32, (tq, tq), 0)
    cc = lax.broadcasted_iota(jnp.int32, (tq, tq), 1)
    causal = cc <= rr
    kblk = kb_ref[pl.ds(t0, tq), :]
    vblk = vb_ref[pl.ds(t0, tq), :]
    for hh in range(2):
        block(hh, kblk, vblk, cn_ref[0, 0, hh:hh + 1, pl.ds(t0, tq)], causal)

    o0 = acc_ref[0] / l_ref[0]
    o1 = acc_ref[1] / l_ref[1]
    o_ref[0] = jnp.where(lane < FOX_DIM, o0, o1).astype(BF16)


def _fox_attn(q, k_new, v_new, c_new, cache):
    b, t, _ = q.shape
    tq = min(256, t)
    pairs = FOX_HEADS // 2
    in_specs = [pl.BlockSpec((1, tq, LANES), lambda i, p, j: (i, j, p)),
                pl.BlockSpec((1, t, LANES), lambda i, p, j: (i, 0, p)),
                pl.BlockSpec((1, t, LANES), lambda i, p, j: (i, 0, p)),
                pl.BlockSpec((1, 1, 2, t), lambda i, p, j: (i, p, 0, 0))]
    args = [q, k_new, v_new, c_new]
    n_cache_blocks, tkc = 0, 0
    if cache is not None:
        kc, vc, cc = cache
        plen = kc.shape[1]
        tkc = min(512, plen)
        n_cache_blocks = plen // tkc
        in_specs += [pl.BlockSpec((1, plen, LANES), lambda i, p, j: (i, 0, p)),
                     pl.BlockSpec((1, plen, LANES), lambda i, p, j: (i, 0, p)),
                     pl.BlockSpec((1, 1, 2, plen), lambda i, p, j: (i, p, 0, 0))]
        args += [kc, vc, cc]
    return pl.pallas_call(
        functools.partial(_fox_kernel, tq=tq, n_cache_blocks=n_cache_blocks, tkc=tkc,
                          single_tile=(t == tq)),
        grid=(b, pairs, t // tq),
        in_specs=in_specs,
        out_specs=pl.BlockSpec((1, tq, LANES), lambda i, p, j: (i, j, p)),
        out_shape=jax.ShapeDtypeStruct((b, t, FOX_W), BF16),
        scratch_shapes=[pltpu.VMEM((t, LANES), BF16), pltpu.VMEM((t, LANES), BF16),
                        pltpu.VMEM((2, tq, 1), F32), pltpu.VMEM((2, tq, 1), F32),
                        pltpu.VMEM((2, tq, LANES), F32)],
        compiler_params=_params("parallel", "parallel", "arbitrary"),
        name="fox_attn",
    )(*args)


_XP_HEAD = SUBLANES


def _mlstm_kernel(mqk_ref, mv_ref, mo_ref, sm_ref, smt_ref, cw_ref, cb_ref, hn_ref,
                  conv0_ref, c0_ref, n0_ref, m0_ref,
                  o_ref, conv_out_ref, c_out_ref, n_out_ref, m_out_ref,
                  xp_ref, c_ref, n_ref, m_ref, *, L):
    ci = pl.program_id(1)
    prev = ML_CONV - 1

    @pl.when(ci == 0)
    def _():
        xp_ref[_XP_HEAD - prev:_XP_HEAD, :] = conv0_ref[0]
        c_ref[...] = c0_ref[0]
        n_ref[...] = n0_ref[0]
        m_ref[...] = m0_ref[0]

    xp_ref[_XP_HEAD:_XP_HEAD + L, :] = mqk_ref[...]
    y = cb_ref[...] + sum(xp_ref[_XP_HEAD - prev + j:_XP_HEAD - prev + j + L, :] * cw_ref[j:j + 1, :]
                          for j in range(ML_CONV))
    qk = y * _sigmoid(y)
    tail = xp_ref[_XP_HEAD + L - prev:_XP_HEAD + L, :]
    conv_out_ref[0] = tail
    xp_ref[_XP_HEAD - prev:_XP_HEAD, :] = tail

    rr = lax.broadcasted_iota(jnp.int32, (L, L), 0)
    cc = lax.broadcasted_iota(jnp.int32, (L, L), 1)
    causal = cc <= rr
    sm = sm_ref[...]
    smt = smt_ref[0]
    hi = lax.Precision.HIGHEST
    bt_cols = jnp.dot(causal.astype(F32), sm, precision=hi, preferred_element_type=F32)
    bt_rows = jnp.dot(smt, (rr <= cc).astype(F32), precision=hi, preferred_element_type=F32)

    for h in range(ML_HEADS):
        sl = slice(h * ML_DIM, (h + 1) * ML_DIM)
        q = qk[:, sl] * (ML_DIM ** -0.5)
        k = qk[:, ML_W + h * ML_DIM:ML_W + (h + 1) * ML_DIM]
        qb, kb = q.astype(BF16), k.astype(BF16)
        v = mv_ref[:, sl]
        it_col = sm[:, _SM_MI + h:_SM_MI + h + 1]
        it_row = smt[_SM_MI + h:_SM_MI + h + 1, :]
        bt_col = bt_cols[:, _SM_MF + h:_SM_MF + h + 1]
        bt_row = bt_rows[_SM_MF + h:_SM_MF + h + 1, :]
        b_end = bt_col[L - 1:L, :]
        m_p = m_ref[h][:, 0:1]
        c_p = c_ref[h]
        n_p = n_ref[h]

        log_w = bt_col - bt_row + it_row
        m_intra = jnp.max(jnp.where(causal, log_w, NEG_BIG), axis=-1, keepdims=True)
        log_inter = bt_col + m_p
        m_t = jnp.maximum(log_inter, m_intra)
        dmat = jnp.where(causal, jnp.exp(log_w - m_t), 0.0)
        s = _dot_nt(qb, kb) * dmat
        w_inter = jnp.exp(log_inter - m_t)
        num = w_inter * _dot_nt(qb, c_p.astype(BF16)) + _dot(s.astype(BF16), v)
        den = w_inter * jnp.sum(q * n_p, axis=-1, keepdims=True) + jnp.sum(s, axis=-1, keepdims=True)
        hout = num / jnp.maximum(jnp.abs(den), jnp.exp(-m_t))

        hg = hout * _sigmoid(mo_ref[:, sl])
        o_ref[:, sl] = _rms(hg, hn_ref[:, sl]).astype(BF16)

        g_col = b_end - bt_col + it_col
        g_max = jnp.max(g_col, axis=0, keepdims=True)
        wg = jnp.exp(g_col - g_max)
        kv_blk = _dot_tn((v.astype(F32) * wg).astype(BF16), kb)
        k_blk = jnp.sum(k * wg, axis=0, keepdims=True)
        m_new = jnp.maximum(b_end + m_p, g_max)
        decay = jnp.exp(b_end + m_p - m_new)
        scale = jnp.exp(g_max - m_new)
        c_new = decay * c_p + scale * kv_blk
        n_new = decay * n_p + scale * k_blk
        c_ref[h] = c_new
        n_ref[h] = n_new
        m_ref[h] = jnp.broadcast_to(m_new, (1, LANES))
        c_out_ref[0, h] = c_new
        n_out_ref[0, h] = n_new
        m_out_ref[0, h] = jnp.broadcast_to(m_new, (1, LANES))


def _mlstm(mqk, mv, mo, sm, smt, W, conv0, c0, n0, m0, b, t):
    L = min(256, t)
    nc = t // L
    row = lambda w: pl.BlockSpec((L, w), lambda i, j: (i * nc + j, 0))
    st = lambda *shape: pl.BlockSpec((1,) + shape, lambda i, j: (i,) + (0,) * len(shape))
    return pl.pallas_call(
        functools.partial(_mlstm_kernel, L=L),
        grid=(b, nc),
        in_specs=[row(2 * ML_W), row(ML_W), row(ML_W), row(LANES),
                  pl.BlockSpec((1, 16, L), lambda i, j: (i, 0, j)),
                  _resident((ML_CONV, 2 * ML_W)), _resident((1, 2 * ML_W)), _resident((1, ML_W)),
                  st(ML_CONV - 1, 2 * ML_W), st(ML_HEADS, ML_DIM, ML_DIM),
                  st(ML_HEADS, 1, ML_DIM), st(ML_HEADS, 1, LANES)],
        out_specs=[row(ML_W), st(ML_CONV - 1, 2 * ML_W), st(ML_HEADS, ML_DIM, ML_DIM),
                   st(ML_HEADS, 1, ML_DIM), st(ML_HEADS, 1, LANES)],
        out_shape=[jax.ShapeDtypeStruct((b * t, ML_W), BF16),
                   jax.ShapeDtypeStruct((b, ML_CONV - 1, 2 * ML_W), F32),
                   jax.ShapeDtypeStruct((b, ML_HEADS, ML_DIM, ML_DIM), F32),
                   jax.ShapeDtypeStruct((b, ML_HEADS, 1, ML_DIM), F32),
                   jax.ShapeDtypeStruct((b, ML_HEADS, 1, LANES), F32)],
        scratch_shapes=[pltpu.VMEM((_XP_HEAD + L, 2 * ML_W), F32),
                        pltpu.VMEM((ML_HEADS, ML_DIM, ML_DIM), F32),
                        pltpu.VMEM((ML_HEADS, 1, ML_DIM), F32),
                        pltpu.VMEM((ML_HEADS, 1, LANES), F32)],
        compiler_params=_params("parallel", "arbitrary"),
        name="mlstm",
    )(mqk, mv, mo, sm, smt, W["mlstm_conv_w"], W["mlstm_conv_b"], W["mlstm_head_norm"],
      conv0, c0, n0, m0)


def _mem_attn_kernel(q_ref, k_ref, v_ref, o_ref):
    for h in range(MEM_HEADS):
        sl = slice(h * MEM_DIM, (h + 1) * MEM_DIM)
        s = _dot_nt(q_ref[:, sl], k_ref[0, :, sl].astype(BF16))
        p = jnp.exp(s - jnp.max(s, axis=-1, keepdims=True))
        o = _dot(p.astype(BF16), v_ref[0, :, sl].astype(BF16)) / jnp.sum(p, axis=-1, keepdims=True)
        o_ref[:, sl] = o.astype(BF16)


def _mem_attn(cq, mem_k, mem_v, b, t):
    tm = min(512, t)
    nt = t // tm
    mlen = mem_k.shape[1]
    kv = pl.BlockSpec((1, mlen, MEM_W), lambda i: (i // nt, 0, 0))
    return pl.pallas_call(
        _mem_attn_kernel,
        grid=(b * nt,),
        in_specs=[pl.BlockSpec((tm, MEM_W), lambda i: (i, 0)), kv, kv],
        out_specs=pl.BlockSpec((tm, MEM_W), lambda i: (i, 0)),
        out_shape=jax.ShapeDtypeStruct((b * t, MEM_W), BF16),
        compiler_params=_params("parallel"),
        name="mem_attn",
    )(cq, mem_k, mem_v)


def _merge_kernel(x_ref, a_ref, b_ref, m_ref, gpre_ref, wg_ref, bg_ref, wa_ref, wb_ref, wm_ref,
                  wo_ref, gpost_ref, o_ref):
    x = x_ref[...]
    xn = _rms(x, gpre_ref[...]).astype(BF16)
    merged = None
    for i, (br_ref, w_ref) in enumerate(((a_ref, wa_ref), (b_ref, wb_ref), (m_ref, wm_ref))):
        sl = slice(i * D_MODEL, (i + 1) * D_MODEL)
        gate = _sigmoid(_dot(xn, wg_ref[:, sl]) + bg_ref[:, sl])
        term = gate * _dot(br_ref[...], w_ref[...])
        merged = term if merged is None else merged + term
    o_ref[...] = x + _rms(_dot(merged.astype(BF16), wo_ref[...]), gpost_ref[...])


def _merge(x2, a, b, m, W):
    n = x2.shape[0]
    tm = min(512, n)
    row = lambda w: pl.BlockSpec((tm, w), lambda i: (i, 0))
    return pl.pallas_call(
        _merge_kernel,
        grid=(n // tm,),
        in_specs=[row(D_MODEL), row(FOX_W), row(ML_W), row(MEM_W), _resident((1, D_MODEL)),
                  _resident((D_MODEL, N_BRANCH * D_MODEL)), _resident((1, N_BRANCH * D_MODEL)),
                  _resident((FOX_W, D_MODEL)), _resident((ML_W, D_MODEL)), _resident((MEM_W, D_MODEL)),
                  _resident((D_MODEL, D_MODEL)), _resident((1, D_MODEL))],
        out_specs=row(D_MODEL),
        out_shape=jax.ShapeDtypeStruct((n, D_MODEL), F32),
        compiler_params=_params("parallel"),
        name="merge",
    )(x2, a, b, m, W["norm_mix_pre"], W["w_gate"], W["b_gate"], W["w_br_a"], W["w_br_b"],
      W["w_br_m"], W["w_out"], W["norm_mix_post"])


_FF_CHUNK = 512


def _ffn_kernel(x_ref, gpre_ref, wup_ref, cw_ref, cb_ref, wdn_ref, gpost_ref, conv0_ref,
                o_ref, conv_out_ref, buf_ref, carry_ref, *, tm):
    ti = pl.program_id(1)
    prev = FFN_CONV - 1

    @pl.when(ti == 0)
    def _():
        carry_ref[SUBLANES - prev:SUBLANES, :] = conv0_ref[0]

    x = x_ref[...]
    xn = _rms(x, gpre_ref[...]).astype(BF16)

    def conv_cols(lo, w):
        up = _dot(xn, wup_ref[:, lo:lo + w])
        buf_ref[SUBLANES - prev:SUBLANES, 0:w] = carry_ref[SUBLANES - prev:SUBLANES, lo:lo + w]
        buf_ref[SUBLANES:SUBLANES + tm, 0:w] = up
        tail = up[tm - prev:tm, :]
        carry_ref[SUBLANES - prev:SUBLANES, lo:lo + w] = tail
        conv_out_ref[0, :, lo:lo + w] = tail
        return cb_ref[:, lo:lo + w] + sum(
            buf_ref[SUBLANES - prev + j:SUBLANES - prev + j + tm, 0:w] * cw_ref[j:j + 1, lo:lo + w]
            for j in range(FFN_CONV))

    acc = None
    for lo in range(0, D_FF, _FF_CHUNK):
        w = min(_FF_CHUNK, D_FF - lo)
        ua = conv_cols(lo, w)
        ub = conv_cols(D_FF + lo, w)
        hid = (jax.nn.gelu(ua, approximate=True) * ub).astype(BF16)
        part = _dot(hid, wdn_ref[lo:lo + w, :])
        acc = part if acc is None else acc + part
    o_ref[...] = x + _rms(acc, gpost_ref[...])


def _ffn(x1, conv0, W, b, t):
    tm = min(512, t)
    nt = t // tm
    row = pl.BlockSpec((tm, D_MODEL), lambda i, j: (i * nt + j, 0))
    st = pl.BlockSpec((1, FFN_CONV - 1, 2 * D_FF), lambda i, j: (i, 0, 0))
    return pl.pallas_call(
        functools.partial(_ffn_kernel, tm=tm),
        grid=(b, nt),
        in_specs=[row, _resident((1, D_MODEL)), _resident((D_MODEL, 2 * D_FF)),
                  _resident((FFN_CONV, 2 * D_FF)), _resident((1, 2 * D_FF)),
                  _resident((D_FF, D_MODEL)), _resident((1, D_MODEL)), st],
        out_specs=[row, st],
        out_shape=[jax.ShapeDtypeStruct((b * t, D_MODEL), F32),
                   jax.ShapeDtypeStruct((b, FFN_CONV - 1, 2 * D_FF), F32)],
        scratch_shapes=[pltpu.VMEM((SUBLANES + tm, _FF_CHUNK), F32),
                        pltpu.VMEM((SUBLANES, 2 * D_FF), F32)],
        compiler_params=_params("parallel", "arbitrary"),
        name="ffn",
    )(x1, W["norm_ffn_pre"], W["w_up"], W["ffn_conv_w"], W["ffn_conv_b"], W["w_down"],
      W["norm_ffn_post"], conv0)


def _prep_weights(norm_mix_pre, w_in, b_in, fox_q_norm, fox_k_norm, mlstm_conv_w, mlstm_conv_b,
                  mlstm_head_norm, w_br_a, w_br_b, w_br_m, w_out, norm_mix_post, norm_ffn_pre,
                  w_up, ffn_conv_w, ffn_conv_b, w_down, norm_ffn_post):
    o_ff, o_mq, o_mi, o_mf, o_mo, o_cq, o_g = 1536, 1544, 3080, 3084, 3088, 3600, 4112

    def pack(a):
        pad = jnp.zeros(a.shape[:-1] + (LANES - FOX_HEADS - 2 * ML_HEADS,), a.dtype)
        small = jnp.concatenate([a[..., o_ff:o_mq], a[..., o_mi:o_mf], a[..., o_mf:o_mo], pad], -1)
        return jnp.concatenate([a[..., :o_ff], a[..., o_mq:o_mi], a[..., o_mo:o_g], small], -1)

    row = lambda v: v.reshape(1, -1).astype(F32)
    head = jnp.arange(FOX_W) // FOX_DIM
    return {
        "norm_mix_pre": row(norm_mix_pre),
        "w_in": pack(w_in).astype(BF16),
        "b_in": row(pack(b_in)),
        "w_gate": w_in[:, o_g:].astype(BF16),
        "b_gate": row(b_in[o_g:]),
        "fox_q_norm": row(jnp.tile(fox_q_norm, FOX_HEADS)),
        "fox_k_norm": row(jnp.tile(fox_k_norm, FOX_HEADS)),
        "head_avg": ((head[:, None] == head[None, :]).astype(F32) / FOX_DIM).astype(BF16),
        "mlstm_conv_w": mlstm_conv_w.astype(F32),
        "mlstm_conv_b": row(mlstm_conv_b),
        "mlstm_head_norm": row(mlstm_head_norm),
        "w_br_a": w_br_a.astype(BF16), "w_br_b": w_br_b.astype(BF16), "w_br_m": w_br_m.astype(BF16),
        "w_out": w_out.astype(BF16),
        "norm_mix_post": row(norm_mix_post),
        "norm_ffn_pre": row(norm_ffn_pre),
        "w_up": w_up.astype(BF16),
        "ffn_conv_w": ffn_conv_w.astype(F32),
        "ffn_conv_b": row(ffn_conv_b),
        "w_down": w_down.astype(BF16),
        "norm_ffn_post": row(norm_ffn_post),
    }


def _rows_layout(a, b, t, r):
    return jnp.transpose(a[:, :r].reshape(b, t, r), (0, 2, 1))


def _layer(x, fox_cache, ml_c0, ml_n0, ml_m0, ml_conv_prev, mem_k, mem_v, ffn_conv_prev, W):
    b, t, d = x.shape
    n = b * t
    x2 = x.reshape(n, d)
    q, fk, fv, mqk, mv, mo, cq, sm = _in_proj(x2, W)

    pairs = FOX_HEADS // 2
    f_rows = _rows_layout(sm, b, t, FOX_HEADS)
    zeros_init = jnp.zeros((b, FOX_HEADS, LANES), F32)
    if fox_cache is None:
        c_new = _cumsum_rows(f_rows, zeros_init)
        cache = None
    else:
        k_cache, v_cache, logf_cache = fox_cache
        plen = k_cache.shape[1]
        c_cache = _cumsum_rows(jnp.transpose(logf_cache.astype(F32), (0, 2, 1)), zeros_init)
        init = jnp.broadcast_to(c_cache[:, :, plen - 1:plen], (b, FOX_HEADS, LANES))
        c_new = _cumsum_rows(f_rows, init)
        cache = (k_cache.reshape(b, plen, FOX_W), v_cache.reshape(b, plen, FOX_W),
                 c_cache.reshape(b, pairs, 2, plen))
    a_out = _fox_attn(q.reshape(b, t, FOX_W), fk.reshape(b, t, FOX_W), fv.reshape(b, t, FOX_W),
                      c_new.reshape(b, pairs, 2, t), cache)

    smt = _rows_layout(sm, b, t, 16)
    b_out, ml_conv_new, c_t, n_t, m_t = _mlstm(
        mqk, mv, mo, sm, smt, W, ml_conv_prev.astype(F32), ml_c0.astype(F32),
        ml_n0.astype(F32).reshape(b, ML_HEADS, 1, ML_DIM),
        jnp.broadcast_to(ml_m0.astype(F32)[:, :, None, None], (b, ML_HEADS, 1, LANES)), b, t)

    mlen = mem_k.shape[1]
    m_out = _mem_attn(cq, mem_k.reshape(b, mlen, MEM_W), mem_v.reshape(b, mlen, MEM_W), b, t)

    x1 = _merge(x2, a_out.reshape(n, FOX_W), b_out, m_out, W)
    y, ffn_conv_new = _ffn(x1, ffn_conv_prev.astype(F32), W, b, t)

    states = (fk.reshape(b, t, FOX_HEADS, FOX_DIM), fv.reshape(b, t, FOX_HEADS, FOX_DIM),
              sm[:, :FOX_HEADS].reshape(b, t, FOX_HEADS), c_t, n_t.reshape(b, ML_HEADS, ML_DIM),
              m_t[:, :, 0, 0], ml_conv_new, ffn_conv_new)
    return y.reshape(b, t, d), states


def kernel(x_prompt, x_sample, cache_fox_k, cache_fox_v, cache_fox_logf, state_mlstm_c, state_mlstm_n, state_mlstm_m, state_mlstm_conv, cache_mem_k, cache_mem_v, state_ffn_conv, mem_prompt, norm_mix_pre, w_in, b_in, fox_q_norm, fox_k_norm, mlstm_conv_w, mlstm_conv_b, mlstm_head_norm, norm_mem, w_mem_kv, w_br_a, w_br_b, w_br_m, w_out, norm_mix_post, norm_ffn_pre, w_up, ffn_conv_w, ffn_conv_b, w_down, norm_ffn_post):
    depth = w_in.shape[0]
    hp, hs = x_prompt, x_sample
    b = x_prompt.shape[0]
    new_p = [[] for _ in range(10)]
    new_s = [[] for _ in range(8)]
    for l in range(depth):
        W = _prep_weights(norm_mix_pre[l], w_in[l], b_in[l], fox_q_norm[l], fox_k_norm[l],
                          mlstm_conv_w[l], mlstm_conv_b[l], mlstm_head_norm[l], w_br_a[l], w_br_b[l],
                          w_br_m[l], w_out[l], norm_mix_post[l], norm_ffn_pre[l], w_up[l],
                          ffn_conv_w[l], ffn_conv_b[l], w_down[l], norm_ffn_post[l])
        mlen = mem_prompt.shape[1]
        kv = _norm_matmul(mem_prompt.reshape(b * mlen, D_MODEL), norm_mem[l].reshape(1, -1).astype(F32),
                          w_mem_kv[l].astype(BF16))
        mem_k_p = kv[:, :MEM_W].reshape(b, mlen, MEM_HEADS, MEM_DIM)
        mem_v_p = kv[:, MEM_W:].reshape(b, mlen, MEM_HEADS, MEM_DIM)
        hp, st_p = _layer(
            hp, None,
            jnp.zeros((b, ML_HEADS, ML_DIM, ML_DIM), F32), jnp.zeros((b, ML_HEADS, ML_DIM), F32),
            jnp.zeros((b, ML_HEADS), F32), jnp.zeros((b, ML_CONV - 1, 2 * ML_W), F32),
            mem_k_p, mem_v_p, jnp.zeros((b, FFN_CONV - 1, 2 * D_FF), F32), W)
        hs, st_s = _layer(
            hs, (cache_fox_k[l], cache_fox_v[l], cache_fox_logf[l]),
            state_mlstm_c[l], state_mlstm_n[l], state_mlstm_m[l], state_mlstm_conv[l],
            cache_mem_k[l], cache_mem_v[l], state_ffn_conv[l], W)
        for acc, a in zip(new_p, st_p + (mem_k_p, mem_v_p)):
            acc.append(a)
        for acc, a in zip(new_s, st_s):
            acc.append(a)
    sp = [jnp.stack(a, axis=0) for a in new_p]
    ss = [jnp.stack(a, axis=0) for a in new_s]
    return (hp, hs, sp[0], sp[1], sp[2], sp[3], sp[4], sp[5], sp[6], sp[7], sp[8], sp[9],
            ss[0], ss[1], ss[2], ss[3], ss[4], ss[5], ss[6], ss[7])
```

```python
import functools

import jax
import jax.numpy as jnp
from jax import lax
from jax.experimental import pallas as pl
from jax.experimental.pallas import tpu as pltpu

F32 = jnp.float32
BF16 = jnp.bfloat16

D_MODEL = 1024
FOX_HEADS = 8
FOX_DIM = 64
ML_HEADS = 4
ML_DIM = 128
ML_CONV = 4
MEM_HEADS = 4
MEM_DIM = 128
D_FF = 2816
FFN_CONV = 3
N_BRANCH = 3
EPS = 1e-6
FOX_W = FOX_HEADS * FOX_DIM
ML_W = ML_HEADS * ML_DIM
MEM_W = MEM_HEADS * MEM_DIM

LANES = 128
SUBLANES = 8
NEG_BIG = -1e30
VMEM_LIMIT = 56 * 1024 * 1024

_C_FQ, _C_FK, _C_FV = 0, 512, 1024
_C_MQK, _C_MV, _C_MO, _C_CQ, _C_SM = 1536, 2560, 3072, 3584, 4096
_IN_COLS = 4224
_SM_FF, _SM_MI, _SM_MF = 0, 8, 12


def _params(*sem):
    return pltpu.CompilerParams(dimension_semantics=sem, vmem_limit_bytes=VMEM_LIMIT)


def _resident(shape):
    nd = len(shape)
    return pl.BlockSpec(shape, lambda *_: (0,) * nd, pipeline_mode=pl.Buffered(1))


def _rms(x, g):
    return x * lax.rsqrt(jnp.mean(x * x, axis=-1, keepdims=True) + EPS) * g


def _dot(a, b):
    return jnp.dot(a, b, preferred_element_type=F32)


def _dot_nt(a, b):
    return lax.dot_general(a, b, (((1,), (1,)), ((), ())), preferred_element_type=F32)


def _dot_tn(a, b):
    return lax.dot_general(a, b, (((0,), (0,)), ((), ())), preferred_element_type=F32)


def _log_sigmoid(z):
    return jnp.minimum(z, 0.0) - jnp.log1p(jnp.exp(-jnp.abs(z)))


def _sigmoid(z):
    return 1.0 / (1.0 + jnp.exp(-z))


def _in_proj_kernel(x_ref, g_ref, w_ref, b_ref, qn_ref, kn_ref, bd_ref,
                    q_ref, fk_ref, fv_ref, mqk_ref, mv_ref, mo_ref, cq_ref, sm_ref):
    xn = _rms(x_ref[...], g_ref[...]).astype(BF16)

    def proj(lo, hi):
        return _dot(xn, w_ref[:, lo:hi]) + b_ref[:, lo:hi]

    def head_norm(z, gn):
        ms = _dot((z * z).astype(BF16), bd_ref[...])
        return z * lax.rsqrt(ms + EPS) * gn

    q_ref[...] = (head_norm(proj(_C_FQ, _C_FK), qn_ref[...]) * (FOX_DIM ** -0.5)).astype(BF16)
    fk_ref[...] = head_norm(proj(_C_FK, _C_FV), kn_ref[...])
    fv_ref[...] = proj(_C_FV, _C_MQK)
    mqk_ref[...] = proj(_C_MQK, _C_MV)
    mv_ref[...] = proj(_C_MV, _C_MO).astype(BF16)
    mo_ref[...] = proj(_C_MO, _C_CQ)
    cq_ref[...] = (proj(_C_CQ, _C_SM) * (MEM_DIM ** -0.5)).astype(BF16)
    z = proj(_C_SM, _IN_COLS)
    lane = lax.broadcasted_iota(jnp.int32, z.shape, 1)
    is_forget = (lane < _SM_MI) | ((lane >= _SM_MF) & (lane < _SM_MF + ML_HEADS))
    sm_ref[...] = jnp.where(is_forget, _log_sigmoid(z), z)


def _in_proj(x2, W):
    n = x2.shape[0]
    tm = min(512, n)
    row = lambda w: pl.BlockSpec((tm, w), lambda i: (i, 0))
    outs = [(FOX_W, BF16), (FOX_W, F32), (FOX_W, F32), (2 * ML_W, F32), (ML_W, BF16),
            (ML_W, F32), (MEM_W, BF16), (LANES, F32)]
    return pl.pallas_call(
        _in_proj_kernel,
        grid=(n // tm,),
        in_specs=[row(D_MODEL), _resident((1, D_MODEL)), _resident((D_MODEL, _IN_COLS)),
                  _resident((1, _IN_COLS)), _resident((1, FOX_W)), _resident((1, FOX_W)),
                  _resident((FOX_W, FOX_W))],
        out_specs=[row(w) for w, _ in outs],
        out_shape=[jax.ShapeDtypeStruct((n, w), dt) for w, dt in outs],
        compiler_params=_params("parallel"),
        name="in_proj",
    )(x2, W["norm_mix_pre"], W["w_in"], W["b_in"], W["fox_q_norm"], W["fox_k_norm"], W["head_avg"])


def _norm_matmul_kernel(x_ref, g_ref, w_ref, o_ref):
    o_ref[...] = _dot(_rms(x_ref[...], g_ref[...]).astype(BF16), w_ref[...])


def _norm_matmul(x2, g, w):
    n, d = x2.shape
    cols = w.shape[1]
    tm = min(512, n)
    return pl.pallas_call(
        _norm_matmul_kernel,
        grid=(n // tm,),
        in_specs=[pl.BlockSpec((tm, d), lambda i: (i, 0)), _resident((1, d)), _resident((d, cols))],
        out_specs=pl.BlockSpec((tm, cols), lambda i: (i, 0)),
        out_shape=jax.ShapeDtypeStruct((n, cols), F32),
        compiler_params=_params("parallel"),
        name="mem_kv",
    )(x2, g, w)


def _cumsum_rows_kernel(f_ref, init_ref, o_ref, carry_ref):
    @pl.when(pl.program_id(0) == 0)
    def _():
        carry_ref[...] = init_ref[...]

    tb = f_ref.shape[-1]
    r = lax.broadcasted_iota(jnp.int32, (tb, tb), 0)
    c = lax.broadcasted_iota(jnp.int32, (tb, tb), 1)
    upper = (r <= c).astype(F32)
    cs = jnp.dot(f_ref[...], upper, precision=lax.Precision.HIGHEST,
                 preferred_element_type=F32) + carry_ref[:, 0:1]
    o_ref[...] = cs
    carry_ref[...] = jnp.broadcast_to(cs[:, tb - 1:tb], carry_ref.shape)


def _cumsum_rows(f_rows, init):
    r, t = f_rows.shape
    tb = min(256, t)
    return pl.pallas_call(
        _cumsum_rows_kernel,
        grid=(t // tb,),
        in_specs=[pl.BlockSpec((r, tb), lambda j: (0, j)), _resident((r, LANES))],
        out_specs=pl.BlockSpec((r, tb), lambda j: (0, j)),
        out_shape=jax.ShapeDtypeStruct((r, t), F32),
        scratch_shapes=[pltpu.VMEM((r, LANES), F32)],
        compiler_params=_params("arbitrary"),
        name="cumsum_rows",
    )(f_rows, init)


def _stack_heads(q):
    lane = lax.broadcasted_iota(jnp.int32, q.shape, 1)
    zero = jnp.zeros_like(q)
    return jnp.concatenate([jnp.where(lane < FOX_DIM, q, zero), jnp.where(lane >= FOX_DIM, q, zero)], axis=0)


def _unstack_heads(o):
    tq = o.shape[0] // 2
    lane = lax.broadcasted_iota(jnp.int32, (tq, LANES), 1)
    return jnp.where(lane < FOX_DIM, o[:tq], o[tq:])


def _attend_block(q2, kblk, vblk, bias, mask, m_ref, l_ref, acc_ref):
    tq = q2.shape[0] // 2
    tk = kblk.shape[0]
    s = _dot_nt(q2, kblk)
    s = jnp.concatenate([s[:tq] + bias[0], s[tq:] + bias[1]], axis=0)
    if mask is not None:
        s = jnp.where(mask, s, NEG_BIG)
    tiles = [s[:, i * LANES:(i + 1) * LANES] for i in range(tk // LANES)] if tk >= LANES else [s]
    m_cur = tiles[0]
    for tl in tiles[1:]:
        m_cur = jnp.maximum(m_cur, tl)
    m_prev = m_ref[...]
    m_new = jnp.maximum(m_prev, jnp.max(m_cur, axis=-1, keepdims=True))
    alpha = jnp.exp(m_prev - m_new)
    if tk >= LANES:
        ps = [jnp.exp(tl - m_new) for tl in tiles]
        l_ref[...] = alpha * l_ref[...] + sum(ps[1:], ps[0])
        p = jnp.concatenate(ps, axis=1) if len(ps) > 1 else ps[0]
    else:
        p = jnp.exp(s - m_new[:, :tk])
        l_scaled = alpha * l_ref[...]
        l_ref[...] = l_scaled
        l_ref[:, :tk] = l_scaled[:, :tk] + p
    acc_ref[...] = alpha * acc_ref[...] + _dot(p.astype(BF16), vblk)
    m_ref[...] = m_new


def _init_stats(m_ref, l_ref, acc_ref):
    m_ref[...] = jnp.full(m_ref.shape, NEG_BIG, F32)
    l_ref[...] = jnp.zeros(l_ref.shape, F32)
    acc_ref[...] = jnp.zeros(acc_ref.shape, F32)


def _causal_stacked(tq):
    rr = lax.broadcasted_iota(jnp.int32, (2 * tq, tq), 0)
    cc = lax.broadcasted_iota(jnp.int32, (2 * tq, tq), 1)
    return cc <= jnp.where(rr >= tq, rr - tq, rr)


def _fox_prompt_kernel(q_ref, kn_ref, vn_ref, cn_ref, o_ref, kb_ref, vb_ref, q2_ref, m_ref, l_ref, acc_ref,
                       *, tq, nq):
    qi = pl.program_id(2)

    @pl.when(qi == 0)
    def _():
        kb_ref[...] = kn_ref[0].astype(BF16)
        vb_ref[...] = vn_ref[0].astype(BF16)

    q2_ref[...] = _stack_heads(q_ref[0])
    _init_stats(m_ref, l_ref, acc_ref)
    t0 = pl.multiple_of(qi * tq, tq) if nq > 1 else 0
    anchor = [cn_ref[0, 0, hh:hh + 1, pl.ds(t0, tq)][:, 0:1] for hh in range(2)]

    def step(s0, tk, mask):
        bias = [anchor[hh] - cn_ref[0, 0, hh:hh + 1, pl.ds(s0, tk)] for hh in range(2)]
        _attend_block(q2_ref[...], kb_ref[pl.ds(s0, tk), :], vb_ref[pl.ds(s0, tk), :], bias, mask,
                      m_ref, l_ref, acc_ref)

    if nq > 1:
        def pair_body(j, carry):
            step(pl.multiple_of(j * (2 * tq), 2 * tq), 2 * tq, None)
            return carry
        lax.fori_loop(0, lax.shift_right_logical(qi, 1), pair_body, 0)

        @pl.when((qi & 1) == 1)
        def _():
            step(pl.multiple_of((qi - 1) * tq, tq), tq, None)

    step(t0, tq, _causal_stacked(tq))
    o = acc_ref[...] / jnp.sum(l_ref[...], axis=-1, keepdims=True)
    o_ref[0] = _unstack_heads(o).astype(BF16)


def _fox_attn_prompt(q, k_new, v_new, c_new):
    b, t, _ = q.shape
    tq = min(256, t)
    return pl.pallas_call(
        functools.partial(_fox_prompt_kernel, tq=tq, nq=t // tq),
        grid=(b, FOX_HEADS // 2, t // tq),
        in_specs=[pl.BlockSpec((1, tq, LANES), lambda i, p, j: (i, j, p)),
                  pl.BlockSpec((1, t, LANES), lambda i, p, j: (i, 0, p)),
                  pl.BlockSpec((1, t, LANES), lambda i, p, j: (i, 0, p)),
                  pl.BlockSpec((1, 1, 2, t), lambda i, p, j: (i, p, 0, 0))],
        out_specs=pl.BlockSpec((1, tq, LANES), lambda i, p, j: (i, j, p)),
        out_shape=jax.ShapeDtypeStruct((b, t, FOX_W), BF16),
        scratch_shapes=[pltpu.VMEM((t, LANES), BF16), pltpu.VMEM((t, LANES), BF16),
                        pltpu.VMEM((2 * tq, LANES), BF16), pltpu.VMEM((2 * tq, LANES), F32),
                        pltpu.VMEM((2 * tq, LANES), F32), pltpu.VMEM((2 * tq, LANES), F32)],
        compiler_params=_params("parallel", "parallel", "arbitrary"),
        name="fox_attn_prompt",
    )(q, k_new, v_new, c_new)


def _fox_cached_kernel(q_ref, kn_ref, vn_ref, cn_ref, kc_ref, vc_ref, cc_ref, o_ref,
                       q2_ref, m_ref, l_ref, acc_ref, *, t, nblk):
    j = pl.program_id(1)
    pairs = FOX_HEADS // 2

    @pl.when(j == 0)
    def _():
        for p in range(pairs):
            q2_ref[p] = _stack_heads(q_ref[0, :, p * LANES:(p + 1) * LANES])
        _init_stats(m_ref, l_ref, acc_ref)

    anchor = [cn_ref[0, h:h + 1, 0:1] for h in range(FOX_HEADS)]
    for p in range(pairs):
        sl = slice(p * LANES, (p + 1) * LANES)
        bias = [anchor[2 * p + hh] - cc_ref[0, 2 * p + hh:2 * p + hh + 1, :] for hh in range(2)]
        _attend_block(q2_ref[p], kc_ref[0, :, sl].astype(BF16), vc_ref[0, :, sl].astype(BF16), bias, None,
                      m_ref.at[p], l_ref.at[p], acc_ref.at[p])

    @pl.when(j == nblk - 1)
    def _():
        causal = _causal_stacked(t)
        for p in range(pairs):
            sl = slice(p * LANES, (p + 1) * LANES)
            bias = [anchor[2 * p + hh] - cn_ref[0, 2 * p + hh:2 * p + hh + 1, :] for hh in range(2)]
            _attend_block(q2_ref[p], kn_ref[0, :, sl].astype(BF16), vn_ref[0, :, sl].astype(BF16), bias, causal,
                          m_ref.at[p], l_ref.at[p], acc_ref.at[p])
            o = acc_ref[p] / jnp.sum(l_ref[p], axis=-1, keepdims=True)
            o_ref[0, :, sl] = _unstack_heads(o).astype(BF16)


def _fox_attn_cached(q, k_new, v_new, c_new, k_cache, v_cache, c_cache):
    b, t, _ = q.shape
    plen = k_cache.shape[1]
    tkc = min(512, plen)
    nblk = plen // tkc
    pairs = FOX_HEADS // 2
    new = lambda w: pl.BlockSpec((1, t, w), lambda i, j: (i, 0, 0))
    return pl.pallas_call(
        functools.partial(_fox_cached_kernel, t=t, nblk=nblk),
        grid=(b, nblk),
        in_specs=[new(FOX_W), new(FOX_W), new(FOX_W),
                  pl.BlockSpec((1, FOX_HEADS, t), lambda i, j: (i, 0, 0)),
                  pl.BlockSpec((1, tkc, FOX_W), lambda i, j: (i, j, 0)),
                  pl.BlockSpec((1, tkc, FOX_W), lambda i, j: (i, j, 0)),
                  pl.BlockSpec((1, FOX_HEADS, tkc), lambda i, j: (i, 0, j))],
        out_specs=new(FOX_W),
        out_shape=jax.ShapeDtypeStruct((b, t, FOX_W), BF16),
        scratch_shapes=[pltpu.VMEM((pairs, 2 * t, LANES), BF16), pltpu.VMEM((pairs, 2 * t, LANES), F32),
                        pltpu.VMEM((pairs, 2 * t, LANES), F32), pltpu.VMEM((pairs, 2 * t, LANES), F32)],
        compiler_params=_params("parallel", "arbitrary"),
        name="fox_attn_cached",
    )(q, k_new, v_new, c_new, k_cache, v_cache, c_cache)


_XP_HEAD = SUBLANES


def _mlstm_kernel(mqk_ref, mv_ref, mo_ref, sm_ref, smt_ref, cw_ref, cb_ref, hn_ref,
                  conv0_ref, c0_ref, n0_ref, m0_ref,
                  o_ref, conv_out_ref, c_out_ref, n_out_ref, m_out_ref,
                  xp_ref, c_ref, n_ref, m_ref, *, L):
    ci = pl.program_id(1)
    prev = ML_CONV - 1

    @pl.when(ci == 0)
    def _():
        xp_ref[_XP_HEAD - prev:_XP_HEAD, :] = conv0_ref[0]
        c_ref[...] = c0_ref[0]
        n_ref[...] = n0_ref[0]
        m_ref[...] = m0_ref[0]

    xp_ref[_XP_HEAD:_XP_HEAD + L, :] = mqk_ref[...]
    y = cb_ref[...] + sum(xp_ref[_XP_HEAD - prev + j:_XP_HEAD - prev + j + L, :] * cw_ref[j:j + 1, :]
                          for j in range(ML_CONV))
    qk = y * _sigmoid(y)
    tail = xp_ref[_XP_HEAD + L - prev:_XP_HEAD + L, :]
    conv_out_ref[0] = tail
    xp_ref[_XP_HEAD - prev:_XP_HEAD, :] = tail

    rr = lax.broadcasted_iota(jnp.int32, (L, L), 0)
    cc = lax.broadcasted_iota(jnp.int32, (L, L), 1)
    causal = cc <= rr
    sm = sm_ref[...]
    smt = smt_ref[0]
    hi = lax.Precision.HIGHEST
    bt_cols = jnp.dot(causal.astype(F32), sm, precision=hi, preferred_element_type=F32)
    bt_rows = jnp.dot(smt, (rr <= cc).astype(F32), precision=hi, preferred_element_type=F32)

    for h in range(ML_HEADS):
        sl = slice(h * ML_DIM, (h + 1) * ML_DIM)
        q = qk[:, sl] * (ML_DIM ** -0.5)
        k = qk[:, ML_W + h * ML_DIM:ML_W + (h + 1) * ML_DIM]
        qb, kb = q.astype(BF16), k.astype(BF16)
        v = mv_ref[:, sl]
        it_col = sm[:, _SM_MI + h:_SM_MI + h + 1]
        it_row = smt[_SM_MI + h:_SM_MI + h + 1, :]
        bt_col = bt_cols[:, _SM_MF + h:_SM_MF + h + 1]
        bt_row = bt_rows[_SM_MF + h:_SM_MF + h + 1, :]
        b_end = bt_col[L - 1:L, :]
        m_p = m_ref[h][:, 0:1]
        c_p = c_ref[h]
        n_p = n_ref[h]

        log_w = bt_col - bt_row + it_row
        m_intra = jnp.max(jnp.where(causal, log_w, NEG_BIG), axis=-1, keepdims=True)
        log_inter = bt_col + m_p
        m_t = jnp.maximum(log_inter, m_intra)
        dmat = jnp.where(causal, jnp.exp(log_w - m_t), 0.0)
        s = _dot_nt(qb, kb) * dmat
        w_inter = jnp.exp(log_inter - m_t)
        num = w_inter * _dot_nt(qb, c_p.astype(BF16)) + _dot(s.astype(BF16), v)
        den = w_inter * jnp.sum(q * n_p, axis=-1, keepdims=True) + jnp.sum(s, axis=-1, keepdims=True)
        hout = num / jnp.maximum(jnp.abs(den), jnp.exp(-m_t))

        hg = hout * _sigmoid(mo_ref[:, sl])
        o_ref[:, sl] = _rms(hg, hn_ref[:, sl]).astype(BF16)

        g_col = b_end - bt_col + it_col
        g_max = jnp.max(g_col, axis=0, keepdims=True)
        wg = jnp.exp(g_col - g_max)
        kv_blk = _dot_tn((v.astype(F32) * wg).astype(BF16), kb)
        k_blk = jnp.sum(k * wg, axis=0, keepdims=True)
        m_new = jnp.maximum(b_end + m_p, g_max)
        decay = jnp.exp(b_end + m_p - m_new)
        scale = jnp.exp(g_max - m_new)
        c_new = decay * c_p + scale * kv_blk
        n_new = decay * n_p + scale * k_blk
        c_ref[h] = c_new
        n_ref[h] = n_new
        m_ref[h] = jnp.broadcast_to(m_new, (1, LANES))
        c_out_ref[0, h] = c_new
        n_out_ref[0, h] = n_new
        m_out_ref[0, h] = jnp.broadcast_to(m_new, (1, LANES))


def _mlstm(mqk, mv, mo, sm, smt, W, conv0, c0, n0, m0, b, t):
    L = min(256, t)
    nc = t // L
    row = lambda w: pl.BlockSpec((L, w), lambda i, j: (i * nc + j, 0))
    st = lambda *shape: pl.BlockSpec((1,) + shape, lambda i, j: (i,) + (0,) * len(shape))
    return pl.pallas_call(
        functools.partial(_mlstm_kernel, L=L),
        grid=(b, nc),
        in_specs=[row(2 * ML_W), row(ML_W), row(ML_W), row(LANES),
                  pl.BlockSpec((1, 16, L), lambda i, j: (i, 0, j)),
                  _resident((ML_CONV, 2 * ML_W)), _resident((1, 2 * ML_W)), _resident((1, ML_W)),
                  st(ML_CONV - 1, 2 * ML_W), st(ML_HEADS, ML_DIM, ML_DIM),
                  st(ML_HEADS, 1, ML_DIM), st(ML_HEADS, 1, LANES)],
        out_specs=[row(ML_W), st(ML_CONV - 1, 2 * ML_W), st(ML_HEADS, ML_DIM, ML_DIM),
                   st(ML_HEADS, 1, ML_DIM), st(ML_HEADS, 1, LANES)],
        out_shape=[jax.ShapeDtypeStruct((b * t, ML_W), BF16),
                   jax.ShapeDtypeStruct((b, ML_CONV - 1, 2 * ML_W), F32),
                   jax.ShapeDtypeStruct((b, ML_HEADS, ML_DIM, ML_DIM), F32),
                   jax.ShapeDtypeStruct((b, ML_HEADS, 1, ML_DIM), F32),
                   jax.ShapeDtypeStruct((b, ML_HEADS, 1, LANES), F32)],
        scratch_shapes=[pltpu.VMEM((_XP_HEAD + L, 2 * ML_W), F32),
                        pltpu.VMEM((ML_HEADS, ML_DIM, ML_DIM), F32),
                        pltpu.VMEM((ML_HEADS, 1, ML_DIM), F32),
                        pltpu.VMEM((ML_HEADS, 1, LANES), F32)],
        compiler_params=_params("parallel", "arbitrary"),
        name="mlstm",
    )(mqk, mv, mo, sm, smt, W["mlstm_conv_w"], W["mlstm_conv_b"], W["mlstm_head_norm"],
      conv0, c0, n0, m0)


def _mem_attn_kernel(q_ref, k_ref, v_ref, o_ref):
    for h in range(MEM_HEADS):
        sl = slice(h * MEM_DIM, (h + 1) * MEM_DIM)
        s = _dot_nt(q_ref[:, sl], k_ref[0, :, sl].astype(BF16))
        p = jnp.exp(s - jnp.max(s, axis=-1, keepdims=True))
        o = _dot(p.astype(BF16), v_ref[0, :, sl].astype(BF16)) / jnp.sum(p, axis=-1, keepdims=True)
        o_ref[:, sl] = o.astype(BF16)


def _mem_attn(cq, mem_k, mem_v, b, t):
    tm = min(512, t)
    nt = t // tm
    mlen = mem_k.shape[1]
    kv = pl.BlockSpec((1, mlen, MEM_W), lambda i: (i // nt, 0, 0))
    return pl.pallas_call(
        _mem_attn_kernel,
        grid=(b * nt,),
        in_specs=[pl.BlockSpec((tm, MEM_W), lambda i: (i, 0)), kv, kv],
        out_specs=pl.BlockSpec((tm, MEM_W), lambda i: (i, 0)),
        out_shape=jax.ShapeDtypeStruct((b * t, MEM_W), BF16),
        compiler_params=_params("parallel"),
        name="mem_attn",
    )(cq, mem_k, mem_v)


def _merge_kernel(x_ref, a_ref, b_ref, m_ref, gpre_ref, wg_ref, bg_ref, wa_ref, wb_ref, wm_ref,
                  wo_ref, gpost_ref, o_ref):
    x = x_ref[...]
    xn = _rms(x, gpre_ref[...]).astype(BF16)
    merged = None
    for i, (br_ref, w_ref) in enumerate(((a_ref, wa_ref), (b_ref, wb_ref), (m_ref, wm_ref))):
        sl = slice(i * D_MODEL, (i + 1) * D_MODEL)
        gate = _sigmoid(_dot(xn, wg_ref[:, sl]) + bg_ref[:, sl])
        term = gate * _dot(br_ref[...], w_ref[...])
        merged = term if merged is None else merged + term
    o_ref[...] = x + _rms(_dot(merged.astype(BF16), wo_ref[...]), gpost_ref[...])


def _merge(x2, a, b, m, W):
    n = x2.shape[0]
    tm = min(512, n)
    row = lambda w: pl.BlockSpec((tm, w), lambda i: (i, 0))
    return pl.pallas_call(
        _merge_kernel,
        grid=(n // tm,),
        in_specs=[row(D_MODEL), row(FOX_W), row(ML_W), row(MEM_W), _resident((1, D_MODEL)),
                  _resident((D_MODEL, N_BRANCH * D_MODEL)), _resident((1, N_BRANCH * D_MODEL)),
                  _resident((FOX_W, D_MODEL)), _resident((ML_W, D_MODEL)), _resident((MEM_W, D_MODEL)),
                  _resident((D_MODEL, D_MODEL)), _resident((1, D_MODEL))],
        out_specs=row(D_MODEL),
        out_shape=jax.ShapeDtypeStruct((n, D_MODEL), F32),
        compiler_params=_params("parallel"),
        name="merge",
    )(x2, a, b, m, W["norm_mix_pre"], W["w_gate"], W["b_gate"], W["w_br_a"], W["w_br_b"],
      W["w_br_m"], W["w_out"], W["norm_mix_post"])


_FF_CHUNK = 512


def _ffn_kernel(x_ref, gpre_ref, wup_ref, cw_ref, cb_ref, wdn_ref, gpost_ref, conv0_ref,
                o_ref, conv_out_ref, buf_ref, carry_ref, *, tm):
    ti = pl.program_id(1)
    prev = FFN_CONV - 1

    @pl.when(ti == 0)
    def _():
        carry_ref[SUBLANES - prev:SUBLANES, :] = conv0_ref[0]

    x = x_ref[...]
    xn = _rms(x, gpre_ref[...]).astype(BF16)

    def conv_cols(lo, w):
        up = _dot(xn, wup_ref[:, lo:lo + w])
        buf_ref[SUBLANES - prev:SUBLANES, 0:w] = carry_ref[SUBLANES - prev:SUBLANES, lo:lo + w]
        buf_ref[SUBLANES:SUBLANES + tm, 0:w] = up
        tail = up[tm - prev:tm, :]
        carry_ref[SUBLANES - prev:SUBLANES, lo:lo + w] = tail
        conv_out_ref[0, :, lo:lo + w] = tail
        return cb_ref[:, lo:lo + w] + sum(
            buf_ref[SUBLANES - prev + j:SUBLANES - prev + j + tm, 0:w] * cw_ref[j:j + 1, lo:lo + w]
            for j in range(FFN_CONV))

    acc = None
    for lo in range(0, D_FF, _FF_CHUNK):
        w = min(_FF_CHUNK, D_FF - lo)
        ua = conv_cols(lo, w)
        ub = conv_cols(D_FF + lo, w)
        hid = (jax.nn.gelu(ua, approximate=True) * ub).astype(BF16)
        part = _dot(hid, wdn_ref[lo:lo + w, :])
        acc = part if acc is None else acc + part
    o_ref[...] = x + _rms(acc, gpost_ref[...])


def _ffn(x1, conv0, W, b, t):
    tm = min(512, t)
    nt = t // tm
    row = pl.BlockSpec((tm, D_MODEL), lambda i, j: (i * nt + j, 0))
    st = pl.BlockSpec((1, FFN_CONV - 1, 2 * D_FF), lambda i, j: (i, 0, 0))
    return pl.pallas_call(
        functools.partial(_ffn_kernel, tm=tm),
        grid=(b, nt),
        in_specs=[row, _resident((1, D_MODEL)), _resident((D_MODEL, 2 * D_FF)),
                  _resident((FFN_CONV, 2 * D_FF)), _resident((1, 2 * D_FF)),
                  _resident((D_FF, D_MODEL)), _resident((1, D_MODEL)), st],
        out_specs=[row, st],
        out_shape=[jax.ShapeDtypeStruct((b * t, D_MODEL), F32),
                   jax.ShapeDtypeStruct((b, FFN_CONV - 1, 2 * D_FF), F32)],
        scratch_shapes=[pltpu.VMEM((SUBLANES + tm, _FF_CHUNK), F32),
                        pltpu.VMEM((SUBLANES, 2 * D_FF), F32)],
        compiler_params=_params("parallel", "arbitrary"),
        name="ffn",
    )(x1, W["norm_ffn_pre"], W["w_up"], W["ffn_conv_w"], W["ffn_conv_b"], W["w_down"],
      W["norm_ffn_post"], conv0)


def _prep_weights(norm_mix_pre, w_in, b_in, fox_q_norm, fox_k_norm, mlstm_conv_w, mlstm_conv_b,
                  mlstm_head_norm, w_br_a, w_br_b, w_br_m, w_out, norm_mix_post, norm_ffn_pre,
                  w_up, ffn_conv_w, ffn_conv_b, w_down, norm_ffn_post):
    o_ff, o_mq, o_mi, o_mf, o_mo, o_cq, o_g = 1536, 1544, 3080, 3084, 3088, 3600, 4112

    def pack(a):
        pad = jnp.zeros(a.shape[:-1] + (LANES - FOX_HEADS - 2 * ML_HEADS,), a.dtype)
        small = jnp.concatenate([a[..., o_ff:o_mq], a[..., o_mi:o_mf], a[..., o_mf:o_mo], pad], -1)
        return jnp.concatenate([a[..., :o_ff], a[..., o_mq:o_mi], a[..., o_mo:o_g], small], -1)

    row = lambda v: v.reshape(1, -1).astype(F32)
    head = jnp.arange(FOX_W) // FOX_DIM
    return {
        "norm_mix_pre": row(norm_mix_pre),
        "w_in": pack(w_in).astype(BF16),
        "b_in": row(pack(b_in)),
        "w_gate": w_in[:, o_g:].astype(BF16),
        "b_gate": row(b_in[o_g:]),
        "fox_q_norm": row(jnp.tile(fox_q_norm, FOX_HEADS)),
        "fox_k_norm": row(jnp.tile(fox_k_norm, FOX_HEADS)),
        "head_avg": ((head[:, None] == head[None, :]).astype(F32) / FOX_DIM).astype(BF16),
        "mlstm_conv_w": mlstm_conv_w.astype(F32),
        "mlstm_conv_b": row(mlstm_conv_b),
        "mlstm_head_norm": row(mlstm_head_norm),
        "w_br_a": w_br_a.astype(BF16), "w_br_b": w_br_b.astype(BF16), "w_br_m": w_br_m.astype(BF16),
        "w_out": w_out.astype(BF16),
        "norm_mix_post": row(norm_mix_post),
        "norm_ffn_pre": row(norm_ffn_pre),
        "w_up": w_up.astype(BF16),
        "ffn_conv_w": ffn_conv_w.astype(F32),
        "ffn_conv_b": row(ffn_conv_b),
        "w_down": w_down.astype(BF16),
        "norm_ffn_post": row(norm_ffn_post),
    }


def _rows_layout(a, b, t, r):
    return jnp.transpose(a[:, :r].reshape(b, t, r), (0, 2, 1))


def _layer(x, fox_cache, ml_c0, ml_n0, ml_m0, ml_conv_prev, mem_k, mem_v, ffn_conv_prev, W):
    b, t, d = x.shape
    n = b * t
    x2 = x.reshape(n, d)
    q, fk, fv, mqk, mv, mo, cq, sm = _in_proj(x2, W)

    pairs = FOX_HEADS // 2
    f_rows = _rows_layout(sm, b, t, FOX_HEADS).reshape(b * FOX_HEADS, t)
    zeros_init = jnp.zeros((b * FOX_HEADS, LANES), F32)
    q3, fk3, fv3 = q.reshape(b, t, FOX_W), fk.reshape(b, t, FOX_W), fv.reshape(b, t, FOX_W)
    if fox_cache is None:
        c_new = _cumsum_rows(f_rows, zeros_init)
        a_out = _fox_attn_prompt(q3, fk3, fv3, c_new.reshape(b, pairs, 2, t))
    else:
        k_cache, v_cache, logf_cache = fox_cache
        plen = k_cache.shape[1]
        logf_rows = jnp.transpose(logf_cache.astype(F32), (0, 2, 1)).reshape(b * FOX_HEADS, plen)
        c_cache = _cumsum_rows(logf_rows, zeros_init)
        c_new = _cumsum_rows(f_rows, jnp.broadcast_to(c_cache[:, plen - 1:plen], (b * FOX_HEADS, LANES)))
        a_out = _fox_attn_cached(q3, fk3, fv3, c_new.reshape(b, FOX_HEADS, t),
                                 k_cache.reshape(b, plen, FOX_W), v_cache.reshape(b, plen, FOX_W),
                                 c_cache.reshape(b, FOX_HEADS, plen))

    smt = _rows_layout(sm, b, t, 16)
    b_out, ml_conv_new, c_t, n_t, m_t = _mlstm(
        mqk, mv, mo, sm, smt, W, ml_conv_prev.astype(F32), ml_c0.astype(F32),
        ml_n0.astype(F32).reshape(b, ML_HEADS, 1, ML_DIM),
        jnp.broadcast_to(ml_m0.astype(F32)[:, :, None, None], (b, ML_HEADS, 1, LANES)), b, t)

    mlen = mem_k.shape[1]
    m_out = _mem_attn(cq, mem_k.reshape(b, mlen, MEM_W), mem_v.reshape(b, mlen, MEM_W), b, t)

    x1 = _merge(x2, a_out.reshape(n, FOX_W), b_out, m_out, W)
    y, ffn_conv_new = _ffn(x1, ffn_conv_prev.astype(F32), W, b, t)

    states = (fk.reshape(b, t, FOX_HEADS, FOX_DIM), fv.reshape(b, t, FOX_HEADS, FOX_DIM),
              sm[:, :FOX_HEADS].reshape(b, t, FOX_HEADS), c_t, n_t.reshape(b, ML_HEADS, ML_DIM),
              m_t[:, :, 0, 0], ml_conv_new, ffn_conv_new)
    return y.reshape(b, t, d), states


def kernel(x_prompt, x_sample, cache_fox_k, cache_fox_v, cache_fox_logf, state_mlstm_c, state_mlstm_n, state_mlstm_m, state_mlstm_conv, cache_mem_k, cache_mem_v, state_ffn_conv, mem_prompt, norm_mix_pre, w_in, b_in, fox_q_norm, fox_k_norm, mlstm_conv_w, mlstm_conv_b, mlstm_head_norm, norm_mem, w_mem_kv, w_br_a, w_br_b, w_br_m, w_out, norm_mix_post, norm_ffn_pre, w_up, ffn_conv_w, ffn_conv_b, w_down, norm_ffn_post):
    depth = w_in.shape[0]
    hp, hs = x_prompt, x_sample
    b = x_prompt.shape[0]
    new_p = [[] for _ in range(10)]
    new_s = [[] for _ in range(8)]
    for l in range(depth):
        W = _prep_weights(norm_mix_pre[l], w_in[l], b_in[l], fox_q_norm[l], fox_k_norm[l],
                          mlstm_conv_w[l], mlstm_conv_b[l], mlstm_head_norm[l], w_br_a[l], w_br_b[l],
                          w_br_m[l], w_out[l], norm_mix_post[l], norm_ffn_pre[l], w_up[l],
                          ffn_conv_w[l], ffn_conv_b[l], w_down[l], norm_ffn_post[l])
        mlen = mem_prompt.shape[1]
        kv = _norm_matmul(mem_prompt.reshape(b * mlen, D_MODEL), norm_mem[l].reshape(1, -1).astype(F32),
                          w_mem_kv[l].astype(BF16))
        mem_k_p = kv[:, :MEM_W].reshape(b, mlen, MEM_HEADS, MEM_DIM)
        mem_v_p = kv[:, MEM_W:].reshape(b, mlen, MEM_HEADS, MEM_DIM)
        hp, st_p = _layer(
            hp, None,
            jnp.zeros((b, ML_HEADS, ML_DIM, ML_DIM), F32), jnp.zeros((b, ML_HEADS, ML_DIM), F32),
            jnp.zeros((b, ML_HEADS), F32), jnp.zeros((b, ML_CONV - 1, 2 * ML_W), F32),
            mem_k_p, mem_v_p, jnp.zeros((b, FFN_CONV - 1, 2 * D_FF), F32), W)
        hs, st_s = _layer(
            hs, (cache_fox_k[l], cache_fox_v[l], cache_fox_logf[l]),
            state_mlstm_c[l], state_mlstm_n[l], state_mlstm_m[l], state_mlstm_conv[l],
            cache_mem_k[l], cache_mem_v[l], state_ffn_conv[l], W)
        for acc, a in zip(new_p, st_p + (mem_k_p, mem_v_p)):
            acc.append(a)
        for acc, a in zip(new_s, st_s):
            acc.append(a)
    sp = [jnp.stack(a, axis=0) for a in new_p]
    ss = [jnp.stack(a, axis=0) for a in new_s]
    return (hp, hs, sp[0], sp[1], sp[2], sp[3], sp[4], sp[5], sp[6], sp[7], sp[8], sp[9],
            ss[0], ss[1], ss[2], ss[3], ss[4], ss[5], ss[6], ss[7])
```

```python
import functools

import jax
import jax.numpy as jnp
from jax import lax
from jax.experimental import pallas as pl
from jax.experimental.pallas import tpu as pltpu

F32 = jnp.float32
BF16 = jnp.bfloat16

D_MODEL = 1024
FOX_HEADS = 8
FOX_DIM = 64
ML_HEADS = 4
ML_DIM = 128
ML_CONV = 4
MEM_HEADS = 4
MEM_DIM = 128
D_FF = 2816
FFN_CONV = 3
N_BRANCH = 3
EPS = 1e-6
FOX_W = FOX_HEADS * FOX_DIM
ML_W = ML_HEADS * ML_DIM
MEM_W = MEM_HEADS * MEM_DIM

LANES = 128
SUBLANES = 8
NEG_BIG = -1e30
LOG2E = 1.4426950408889634
VMEM_LIMIT = 56 * 1024 * 1024

_C_FQ, _C_FK, _C_FV = 0, 512, 1024
_C_MQK, _C_MV, _C_MO, _C_CQ, _C_SM = 1536, 2560, 3072, 3584, 4096
_IN_COLS = 4224
_SM_FF, _SM_MI, _SM_MF = 0, 8, 12


def _params(*sem):
    return pltpu.CompilerParams(dimension_semantics=sem, vmem_limit_bytes=VMEM_LIMIT)


def _resident(shape):
    nd = len(shape)
    return pl.BlockSpec(shape, lambda *_: (0,) * nd, pipeline_mode=pl.Buffered(1))


def _rms(x, g):
    return x * lax.rsqrt(jnp.mean(x * x, axis=-1, keepdims=True) + EPS) * g


def _dot(a, b):
    return jnp.dot(a, b, preferred_element_type=F32)


def _dot_nt(a, b):
    return lax.dot_general(a, b, (((1,), (1,)), ((), ())), preferred_element_type=F32)


def _dot_tn(a, b):
    return lax.dot_general(a, b, (((0,), (0,)), ((), ())), preferred_element_type=F32)


def _log_sigmoid(z):
    return jnp.minimum(z, 0.0) - jnp.log1p(jnp.exp(-jnp.abs(z)))


def _sigmoid(z):
    return 1.0 / (1.0 + jnp.exp(-z))


def _in_proj_kernel(x_ref, g_ref, w_ref, b_ref, qn_ref, kn_ref, bd_ref,
                    q_ref, kb_ref, vb_ref, fk_ref, fv_ref, mqk_ref, mv_ref, mo_ref, cq_ref, sm_ref):
    xn = _rms(x_ref[...], g_ref[...]).astype(BF16)

    def proj(lo, hi):
        return _dot(xn, w_ref[:, lo:hi]) + b_ref[:, lo:hi]

    def head_norm(z, gn):
        ms = _dot((z * z).astype(BF16), bd_ref[...])
        return z * lax.rsqrt(ms + EPS) * gn

    def store_heads(dense_ref, heads_ref, val):
        dense_ref[...] = val.astype(BF16)
        for h in range(FOX_HEADS):
            heads_ref[:, h, :] = val[:, h * FOX_DIM:(h + 1) * FOX_DIM]

    q_ref[...] = (head_norm(proj(_C_FQ, _C_FK), qn_ref[...]) * (FOX_DIM ** -0.5 * LOG2E)).astype(BF16)
    store_heads(kb_ref, fk_ref, head_norm(proj(_C_FK, _C_FV), kn_ref[...]))
    store_heads(vb_ref, fv_ref, proj(_C_FV, _C_MQK))
    mqk_ref[...] = proj(_C_MQK, _C_MV)
    mv_ref[...] = proj(_C_MV, _C_MO).astype(BF16)
    mo_ref[...] = proj(_C_MO, _C_CQ)
    cq_ref[...] = (proj(_C_CQ, _C_SM) * (MEM_DIM ** -0.5)).astype(BF16)
    z = proj(_C_SM, _IN_COLS)
    lane = lax.broadcasted_iota(jnp.int32, z.shape, 1)
    is_forget = (lane < _SM_MI) | ((lane >= _SM_MF) & (lane < _SM_MF + ML_HEADS))
    sm_ref[...] = jnp.where(is_forget, _log_sigmoid(z), z)


def _in_proj(x2, W):
    n = x2.shape[0]
    tm = min(512, n)
    row = lambda w: pl.BlockSpec((tm, w), lambda i: (i, 0))
    heads = pl.BlockSpec((tm, FOX_HEADS, FOX_DIM), lambda i: (i, 0, 0))
    heads_shape = jax.ShapeDtypeStruct((n, FOX_HEADS, FOX_DIM), F32)
    outs = [(FOX_W, BF16), (FOX_W, BF16), (FOX_W, BF16), None, None, (2 * ML_W, F32), (ML_W, BF16),
            (ML_W, F32), (MEM_W, BF16), (LANES, F32)]
    return pl.pallas_call(
        _in_proj_kernel,
        grid=(n // tm,),
        in_specs=[row(D_MODEL), _resident((1, D_MODEL)), _resident((D_MODEL, _IN_COLS)),
                  _resident((1, _IN_COLS)), _resident((1, FOX_W)), _resident((1, FOX_W)),
                  _resident((FOX_W, FOX_W))],
        out_specs=[heads if o is None else row(o[0]) for o in outs],
        out_shape=[heads_shape if o is None else jax.ShapeDtypeStruct((n, o[0]), o[1]) for o in outs],
        compiler_params=_params("parallel"),
        name="in_proj",
    )(x2, W["norm_mix_pre"], W["w_in"], W["b_in"], W["fox_q_norm"], W["fox_k_norm"], W["head_avg"])


def _norm_matmul_kernel(x_ref, g_ref, w_ref, o_ref):
    o_ref[...] = _dot(_rms(x_ref[...], g_ref[...]).astype(BF16), w_ref[...])


def _norm_matmul(x2, g, w):
    n, d = x2.shape
    cols = w.shape[1]
    tm = min(512, n)
    return pl.pallas_call(
        _norm_matmul_kernel,
        grid=(n // tm,),
        in_specs=[pl.BlockSpec((tm, d), lambda i: (i, 0)), _resident((1, d)), _resident((d, cols))],
        out_specs=pl.BlockSpec((tm, cols), lambda i: (i, 0)),
        out_shape=jax.ShapeDtypeStruct((n, cols), F32),
        compiler_params=_params("parallel"),
        name="mem_kv",
    )(x2, g, w)


def _cumsum_rows_kernel(f_ref, init_ref, o_ref, carry_ref):
    @pl.when(pl.program_id(0) == 0)
    def _():
        carry_ref[...] = init_ref[...]

    tb = f_ref.shape[-1]
    r = lax.broadcasted_iota(jnp.int32, (tb, tb), 0)
    c = lax.broadcasted_iota(jnp.int32, (tb, tb), 1)
    upper = (r <= c).astype(F32)
    cs = jnp.dot(f_ref[...] * LOG2E, upper, precision=lax.Precision.HIGHEST,
                 preferred_element_type=F32) + carry_ref[:, 0:1]
    o_ref[...] = cs
    carry_ref[...] = jnp.broadcast_to(cs[:, tb - 1:tb], carry_ref.shape)


def _cumsum_rows(f_rows, init):
    r, t = f_rows.shape
    tb = min(256, t)
    return pl.pallas_call(
        _cumsum_rows_kernel,
        grid=(t // tb,),
        in_specs=[pl.BlockSpec((r, tb), lambda j: (0, j)), _resident((r, LANES))],
        out_specs=pl.BlockSpec((r, tb), lambda j: (0, j)),
        out_shape=jax.ShapeDtypeStruct((r, t), F32),
        scratch_shapes=[pltpu.VMEM((r, LANES), F32)],
        compiler_params=_params("arbitrary"),
        name="cumsum_rows",
    )(f_rows, init)


def _stack_heads(q):
    lane = lax.broadcasted_iota(jnp.int32, q.shape, 1)
    zero = jnp.zeros_like(q)
    return jnp.concatenate([jnp.where(lane < FOX_DIM, q, zero), jnp.where(lane >= FOX_DIM, q, zero)], axis=0)


def _unstack_heads(o):
    tq = o.shape[0] // 2
    lane = lax.broadcasted_iota(jnp.int32, (tq, LANES), 1)
    return jnp.where(lane < FOX_DIM, o[:tq], o[tq:])


def _softmax_update(s, bias, mask, m_ref, l_ref):
    tq = s.shape[0] // len(bias)
    tk = s.shape[1]
    parts = [s[i * tq:(i + 1) * tq] + b for i, b in enumerate(bias)]
    s = jnp.concatenate(parts, axis=0) if len(parts) > 1 else parts[0]
    if mask is not None:
        s = jnp.where(mask, s, NEG_BIG)
    tiles = [s[:, i * LANES:(i + 1) * LANES] for i in range(tk // LANES)] if tk >= LANES else [s]
    m_cur = tiles[0]
    for tl in tiles[1:]:
        m_cur = jnp.maximum(m_cur, tl)
    m_prev = m_ref[...]
    m_new = jnp.maximum(m_prev, jnp.max(m_cur, axis=-1, keepdims=True))
    alpha = jnp.exp2(m_prev - m_new)
    if tk >= LANES:
        ps = [jnp.exp2(tl - m_new) for tl in tiles]
        l_ref[...] = alpha * l_ref[...] + sum(ps[1:], ps[0])
        p = jnp.concatenate(ps, axis=1) if len(ps) > 1 else ps[0]
    else:
        p = jnp.exp2(s - m_new[:, :tk])
        l_scaled = alpha * l_ref[...]
        l_ref[...] = l_scaled
        l_ref[:, :tk] = l_scaled[:, :tk] + p
    m_ref[...] = m_new
    return p.astype(BF16), alpha


def _init_stats(m_ref, l_ref, acc_ref):
    m_ref[...] = jnp.full(m_ref.shape, NEG_BIG, F32)
    l_ref[...] = jnp.zeros(l_ref.shape, F32)
    acc_ref[...] = jnp.zeros(acc_ref.shape, F32)


def _causal_stacked(tq):
    rr = lax.broadcasted_iota(jnp.int32, (2 * tq, tq), 0)
    cc = lax.broadcasted_iota(jnp.int32, (2 * tq, tq), 1)
    return cc <= jnp.where(rr >= tq, rr - tq, rr)


def _fox_prompt_kernel(q_ref, kb_ref, vb_ref, cn_ref, o_ref, q2_ref, m_ref, l_ref, acc_ref,
                       s_ref, p_ref, *, tq, nq):
    qi = pl.program_id(2)
    q2_ref[...] = _stack_heads(q_ref[0])
    _init_stats(m_ref, l_ref, acc_ref)
    t0 = pl.multiple_of(qi * tq, tq) if nq > 1 else 0
    anchor = [cn_ref[0, 0, hh:hh + 1, pl.ds(t0, tq)][:, 0:1] for hh in range(2)]

    def start(j):
        return pl.multiple_of(j * tq, tq) if nq > 1 else 0

    def scores(j):
        return _dot_nt(q2_ref[...], kb_ref[0, pl.ds(start(j), tq), :])

    def softmax(s, j, mask):
        bias = [anchor[hh] - cn_ref[0, 0, hh:hh + 1, pl.ds(start(j), tq)] for hh in range(2)]
        return _softmax_update(s, bias, mask, m_ref, l_ref)

    def weighted_values(p, j):
        return _dot(p, vb_ref[0, pl.ds(start(j), tq), :])

    def stage(j, slot, mask):
        s_ref[1 - slot] = scores(j + 1)
        pv_prev = weighted_values(p_ref[1 - slot], jnp.maximum(j - 1, 0))
        p_cur, alpha = softmax(s_ref[slot], j, mask)
        p_ref[slot] = p_cur
        acc_ref[...] = alpha * (acc_ref[...] + pv_prev)

    p_ref[...] = jnp.zeros(p_ref.shape, BF16)
    if nq > 1:
        odd = (qi & 1) == 1

        @pl.when(odd)
        def _():
            s_ref[1] = scores(0)
            stage(0, 1, None)

        @pl.when(jnp.logical_not(odd))
        def _():
            s_ref[0] = scores(0)

        j0 = qi & 1

        def body(i, carry):
            stage(j0 + 2 * i, 0, None)
            stage(j0 + 2 * i + 1, 1, None)
            return carry
        lax.fori_loop(0, lax.shift_right_logical(qi, 1), body, 0)
    else:
        s_ref[0] = scores(0)
    pv_prev = weighted_values(p_ref[1], jnp.maximum(qi - 1, 0))
    p_cur, alpha = softmax(s_ref[0], qi, _causal_stacked(tq))
    acc = alpha * (acc_ref[...] + pv_prev) + weighted_values(p_cur, qi)
    o = acc / jnp.sum(l_ref[...], axis=-1, keepdims=True)
    o_ref[0] = _unstack_heads(o).astype(BF16)


def _fox_attn_prompt(q, k_new, v_new, c_new):
    b, t, _ = q.shape
    tq = min(512, t)
    return pl.pallas_call(
        functools.partial(_fox_prompt_kernel, tq=tq, nq=t // tq),
        grid=(b, FOX_HEADS // 2, t // tq),
        in_specs=[pl.BlockSpec((1, tq, LANES), lambda i, p, j: (i, j, p)),
                  pl.BlockSpec((1, t, LANES), lambda i, p, j: (i, 0, p)),
                  pl.BlockSpec((1, t, LANES), lambda i, p, j: (i, 0, p)),
                  pl.BlockSpec((1, 1, 2, t), lambda i, p, j: (i, p, 0, 0))],
        out_specs=pl.BlockSpec((1, tq, LANES), lambda i, p, j: (i, j, p)),
        out_shape=jax.ShapeDtypeStruct((b, t, FOX_W), BF16),
        scratch_shapes=[pltpu.VMEM((2 * tq, LANES), BF16), pltpu.VMEM((2 * tq, LANES), F32),
                        pltpu.VMEM((2 * tq, LANES), F32), pltpu.VMEM((2 * tq, LANES), F32),
                        pltpu.VMEM((2, 2 * tq, tq), F32), pltpu.VMEM((2, 2 * tq, tq), BF16)],
        compiler_params=_params("parallel", "parallel", "arbitrary"),
        name="fox_attn_prompt",
    )(q, k_new, v_new, c_new)


def _fox_cached_kernel(q_ref, kn_ref, vn_ref, cn_ref, kc_ref, vc_ref, cc_ref, o_ref,
                       qh_ref, m_ref, l_ref, acc_ref, *, t, nblk, tkc):
    j = pl.program_id(1)
    head = lambda h: slice(h * FOX_DIM, (h + 1) * FOX_DIM)

    @pl.when(j == 0)
    def _():
        for h in range(FOX_HEADS):
            qh_ref[h] = q_ref[0, :, head(h)]
        _init_stats(m_ref, l_ref, acc_ref)

    def attend(h, kblk, vblk, c_row, mask):
        bias = [cn_ref[0, h:h + 1, 0:1] - c_row]
        p, alpha = _softmax_update(_dot_nt(qh_ref[h], kblk), bias, mask, m_ref.at[h], l_ref.at[h])
        acc_ref[h] = alpha[:, :FOX_DIM] * acc_ref[h] + _dot(p, vblk)

    for h in range(FOX_HEADS):
        rows = pl.ds(h, tkc, stride=FOX_HEADS)
        attend(h, kc_ref[rows, :].astype(BF16), vc_ref[rows, :].astype(BF16), cc_ref[0, h:h + 1, :], None)

    @pl.when(j == nblk - 1)
    def _():
        rr = lax.broadcasted_iota(jnp.int32, (t, t), 0)
        cc = lax.broadcasted_iota(jnp.int32, (t, t), 1)
        for h in range(FOX_HEADS):
            attend(h, kn_ref[0, :, head(h)], vn_ref[0, :, head(h)], cn_ref[0, h:h + 1, :], cc <= rr)
            o = acc_ref[h] / jnp.sum(l_ref[h], axis=-1, keepdims=True)
            o_ref[0, :, head(h)] = o.astype(BF16)


def _fox_attn_cached(q, k_new, v_new, c_new, k_cache, v_cache, c_cache):
    b, t, _ = q.shape
    plen = c_cache.shape[-1]
    tkc = min(512, plen)
    nblk = plen // tkc
    new = pl.BlockSpec((1, t, FOX_W), lambda i, j: (i, 0, 0))
    cache = pl.BlockSpec((tkc * FOX_HEADS, FOX_DIM), lambda i, j: (i * nblk + j, 0))
    return pl.pallas_call(
        functools.partial(_fox_cached_kernel, t=t, nblk=nblk, tkc=tkc),
        grid=(b, nblk),
        in_specs=[new, new, new, pl.BlockSpec((1, FOX_HEADS, t), lambda i, j: (i, 0, 0)),
                  cache, cache, pl.BlockSpec((1, FOX_HEADS, tkc), lambda i, j: (i, 0, j))],
        out_specs=new,
        out_shape=jax.ShapeDtypeStruct((b, t, FOX_W), BF16),
        scratch_shapes=[pltpu.VMEM((FOX_HEADS, t, FOX_DIM), BF16), pltpu.VMEM((FOX_HEADS, t, LANES), F32),
                        pltpu.VMEM((FOX_HEADS, t, LANES), F32), pltpu.VMEM((FOX_HEADS, t, FOX_DIM), F32)],
        compiler_params=_params("parallel", "arbitrary"),
        name="fox_attn_cached",
    )(q, k_new, v_new, c_new, k_cache, v_cache, c_cache)


_XP_HEAD = SUBLANES


def _mlstm_kernel(mqk_ref, mv_ref, mo_ref, sm_ref, smt_ref, cw_ref, cb_ref, hn_ref,
                  conv0_ref, c0_ref, n0_ref, m0_ref,
                  o_ref, conv_out_ref, c_out_ref, n_out_ref, m_out_ref,
                  xp_ref, c_ref, n_ref, m_ref, *, L):
    ci = pl.program_id(1)
    prev = ML_CONV - 1

    @pl.when(ci == 0)
    def _():
        xp_ref[_XP_HEAD - prev:_XP_HEAD, :] = conv0_ref[0]
        c_ref[...] = c0_ref[0]
        n_ref[...] = n0_ref[0]
        m_ref[...] = m0_ref[0]

    xp_ref[_XP_HEAD:_XP_HEAD + L, :] = mqk_ref[...]
    y = cb_ref[...] + sum(xp_ref[_XP_HEAD - prev + j:_XP_HEAD - prev + j + L, :] * cw_ref[j:j + 1, :]
                          for j in range(ML_CONV))
    qk = y * _sigmoid(y)
    tail = xp_ref[_XP_HEAD + L - prev:_XP_HEAD + L, :]
    conv_out_ref[0] = tail
    xp_ref[_XP_HEAD - prev:_XP_HEAD, :] = tail

    rr = lax.broadcasted_iota(jnp.int32, (L, L), 0)
    cc = lax.broadcasted_iota(jnp.int32, (L, L), 1)
    causal = cc <= rr
    sm = sm_ref[...]
    smt = smt_ref[0]
    hi = lax.Precision.HIGHEST
    bt_cols = jnp.dot(causal.astype(F32), sm, precision=hi, preferred_element_type=F32)
    bt_rows = jnp.dot(smt, (rr <= cc).astype(F32), precision=hi, preferred_element_type=F32)

    for h in range(ML_HEADS):
        sl = slice(h * ML_DIM, (h + 1) * ML_DIM)
        q = qk[:, sl] * (ML_DIM ** -0.5)
        k = qk[:, ML_W + h * ML_DIM:ML_W + (h + 1) * ML_DIM]
        qb, kb = q.astype(BF16), k.astype(BF16)
        v = mv_ref[:, sl]
        it_col = sm[:, _SM_MI + h:_SM_MI + h + 1]
        it_row = smt[_SM_MI + h:_SM_MI + h + 1, :]
        bt_col = bt_cols[:, _SM_MF + h:_SM_MF + h + 1]
        bt_row = bt_rows[_SM_MF + h:_SM_MF + h + 1, :]
        b_end = bt_col[L - 1:L, :]
        m_p = m_ref[h][:, 0:1]
        c_p = c_ref[h]
        n_p = n_ref[h]

        log_w = bt_col - bt_row + it_row
        m_intra = jnp.max(jnp.where(causal, log_w, NEG_BIG), axis=-1, keepdims=True)
        log_inter = bt_col + m_p
        m_t = jnp.maximum(log_inter, m_intra)
        dmat = jnp.where(causal, jnp.exp(log_w - m_t), 0.0)
        s = _dot_nt(qb, kb) * dmat
        w_inter = jnp.exp(log_inter - m_t)
        num = w_inter * _dot_nt(qb, c_p.astype(BF16)) + _dot(s.astype(BF16), v)
        den = w_inter * jnp.sum(q * n_p, axis=-1, keepdims=True) + jnp.sum(s, axis=-1, keepdims=True)
        hout = num / jnp.maximum(jnp.abs(den), jnp.exp(-m_t))

        hg = hout * _sigmoid(mo_ref[:, sl])
        o_ref[:, sl] = _rms(hg, hn_ref[:, sl]).astype(BF16)

        g_col = b_end - bt_col + it_col
        g_max = jnp.max(g_col, axis=0, keepdims=True)
        wg = jnp.exp(g_col - g_max)
        kv_blk = _dot_tn((v.astype(F32) * wg).astype(BF16), kb)
        k_blk = jnp.sum(k * wg, axis=0, keepdims=True)
        m_new = jnp.maximum(b_end + m_p, g_max)
        decay = jnp.exp(b_end + m_p - m_new)
        scale = jnp.exp(g_max - m_new)
        c_new = decay * c_p + scale * kv_blk
        n_new = decay * n_p + scale * k_blk
        c_ref[h] = c_new
        n_ref[h] = n_new
        m_ref[h] = jnp.broadcast_to(m_new, (1, LANES))
        c_out_ref[0, h] = c_new
        n_out_ref[0, h] = n_new
        m_out_ref[0, h] = jnp.broadcast_to(m_new, (1, LANES))


def _mlstm(mqk, mv, mo, sm, smt, W, conv0, c0, n0, m0, b, t):
    L = min(256, t)
    nc = t // L
    row = lambda w: pl.BlockSpec((L, w), lambda i, j: (i * nc + j, 0))
    st = lambda *shape: pl.BlockSpec((1,) + shape, lambda i, j: (i,) + (0,) * len(shape))
    return pl.pallas_call(
        functools.partial(_mlstm_kernel, L=L),
        grid=(b, nc),
        in_specs=[row(2 * ML_W), row(ML_W), row(ML_W), row(LANES),
                  pl.BlockSpec((1, 16, L), lambda i, j: (i, 0, j)),
                  _resident((ML_CONV, 2 * ML_W)), _resident((1, 2 * ML_W)), _resident((1, ML_W)),
                  st(ML_CONV - 1, 2 * ML_W), st(ML_HEADS, ML_DIM, ML_DIM),
                  st(ML_HEADS, 1, ML_DIM), st(ML_HEADS, 1, LANES)],
        out_specs=[row(ML_W), st(ML_CONV - 1, 2 * ML_W), st(ML_HEADS, ML_DIM, ML_DIM),
                   st(ML_HEADS, 1, ML_DIM), st(ML_HEADS, 1, LANES)],
        out_shape=[jax.ShapeDtypeStruct((b * t, ML_W), BF16),
                   jax.ShapeDtypeStruct((b, ML_CONV - 1, 2 * ML_W), F32),
                   jax.ShapeDtypeStruct((b, ML_HEADS, ML_DIM, ML_DIM), F32),
                   jax.ShapeDtypeStruct((b, ML_HEADS, 1, ML_DIM), F32),
                   jax.ShapeDtypeStruct((b, ML_HEADS, 1, LANES), F32)],
        scratch_shapes=[pltpu.VMEM((_XP_HEAD + L, 2 * ML_W), F32),
                        pltpu.VMEM((ML_HEADS, ML_DIM, ML_DIM), F32),
                        pltpu.VMEM((ML_HEADS, 1, ML_DIM), F32),
                        pltpu.VMEM((ML_HEADS, 1, LANES), F32)],
        compiler_params=_params("parallel", "arbitrary"),
        name="mlstm",
    )(mqk, mv, mo, sm, smt, W["mlstm_conv_w"], W["mlstm_conv_b"], W["mlstm_head_norm"],
      conv0, c0, n0, m0)


def _mem_attn_kernel(q_ref, k_ref, v_ref, o_ref):
    for h in range(MEM_HEADS):
        sl = slice(h * MEM_DIM, (h + 1) * MEM_DIM)
        s = _dot_nt(q_ref[:, sl], k_ref[0, :, sl].astype(BF16))
        p = jnp.exp(s - jnp.max(s, axis=-1, keepdims=True))
        o = _dot(p.astype(BF16), v_ref[0, :, sl].astype(BF16)) / jnp.sum(p, axis=-1, keepdims=True)
        o_ref[:, sl] = o.astype(BF16)


def _mem_attn(cq, mem_k, mem_v, b, t):
    tm = min(512, t)
    nt = t // tm
    mlen = mem_k.shape[1]
    kv = pl.BlockSpec((1, mlen, MEM_W), lambda i: (i // nt, 0, 0))
    return pl.pallas_call(
        _mem_attn_kernel,
        grid=(b * nt,),
        in_specs=[pl.BlockSpec((tm, MEM_W), lambda i: (i, 0)), kv, kv],
        out_specs=pl.BlockSpec((tm, MEM_W), lambda i: (i, 0)),
        out_shape=jax.ShapeDtypeStruct((b * t, MEM_W), BF16),
        compiler_params=_params("parallel"),
        name="mem_attn",
    )(cq, mem_k, mem_v)


def _merge_kernel(x_ref, a_ref, b_ref, m_ref, gpre_ref, wg_ref, bg_ref, wa_ref, wb_ref, wm_ref,
                  wo_ref, gpost_ref, o_ref):
    x = x_ref[...]
    xn = _rms(x, gpre_ref[...]).astype(BF16)
    merged = None
    for i, (br_ref, w_ref) in enumerate(((a_ref, wa_ref), (b_ref, wb_ref), (m_ref, wm_ref))):
        sl = slice(i * D_MODEL, (i + 1) * D_MODEL)
        gate = _sigmoid(_dot(xn, wg_ref[:, sl]) + bg_ref[:, sl])
        term = gate * _dot(br_ref[...], w_ref[...])
        merged = term if merged is None else merged + term
    o_ref[...] = x + _rms(_dot(merged.astype(BF16), wo_ref[...]), gpost_ref[...])


def _merge(x2, a, b, m, W):
    n = x2.shape[0]
    tm = min(512, n)
    row = lambda w: pl.BlockSpec((tm, w), lambda i: (i, 0))
    return pl.pallas_call(
        _merge_kernel,
        grid=(n // tm,),
        in_specs=[row(D_MODEL), row(FOX_W), row(ML_W), row(MEM_W), _resident((1, D_MODEL)),
                  _resident((D_MODEL, N_BRANCH * D_MODEL)), _resident((1, N_BRANCH * D_MODEL)),
                  _resident((FOX_W, D_MODEL)), _resident((ML_W, D_MODEL)), _resident((MEM_W, D_MODEL)),
                  _resident((D_MODEL, D_MODEL)), _resident((1, D_MODEL))],
        out_specs=row(D_MODEL),
        out_shape=jax.ShapeDtypeStruct((n, D_MODEL), F32),
        compiler_params=_params("parallel"),
        name="merge",
    )(x2, a, b, m, W["norm_mix_pre"], W["w_gate"], W["b_gate"], W["w_br_a"], W["w_br_b"],
      W["w_br_m"], W["w_out"], W["norm_mix_post"])


_FF_CHUNK = 512


def _ffn_kernel(x_ref, gpre_ref, wup_ref, cw_ref, cb_ref, wdn_ref, gpost_ref, conv0_ref,
                o_ref, conv_out_ref, buf_ref, carry_ref, *, tm):
    ti = pl.program_id(1)
    prev = FFN_CONV - 1

    @pl.when(ti == 0)
    def _():
        carry_ref[SUBLANES - prev:SUBLANES, :] = conv0_ref[0]

    x = x_ref[...]
    xn = _rms(x, gpre_ref[...]).astype(BF16)

    def conv_cols(lo, w):
        up = _dot(xn, wup_ref[:, lo:lo + w])
        buf_ref[SUBLANES - prev:SUBLANES, 0:w] = carry_ref[SUBLANES - prev:SUBLANES, lo:lo + w]
        buf_ref[SUBLANES:SUBLANES + tm, 0:w] = up
        tail = up[tm - prev:tm, :]
        carry_ref[SUBLANES - prev:SUBLANES, lo:lo + w] = tail
        conv_out_ref[0, :, lo:lo + w] = tail
        return cb_ref[:, lo:lo + w] + sum(
            buf_ref[SUBLANES - prev + j:SUBLANES - prev + j + tm, 0:w] * cw_ref[j:j + 1, lo:lo + w]
            for j in range(FFN_CONV))

    acc = None
    for lo in range(0, D_FF, _FF_CHUNK):
        w = min(_FF_CHUNK, D_FF - lo)
        ua = conv_cols(lo, w)
        ub = conv_cols(D_FF + lo, w)
        hid = (jax.nn.gelu(ua, approximate=True) * ub).astype(BF16)
        part = _dot(hid, wdn_ref[lo:lo + w, :])
        acc = part if acc is None else acc + part
    o_ref[...] = x + _rms(acc, gpost_ref[...])


def _ffn(x1, conv0, W, b, t):
    tm = min(512, t)
    nt = t // tm
    row = pl.BlockSpec((tm, D_MODEL), lambda i, j: (i * nt + j, 0))
    st = pl.BlockSpec((1, FFN_CONV - 1, 2 * D_FF), lambda i, j: (i, 0, 0))
    return pl.pallas_call(
        functools.partial(_ffn_kernel, tm=tm),
        grid=(b, nt),
        in_specs=[row, _resident((1, D_MODEL)), _resident((D_MODEL, 2 * D_FF)),
                  _resident((FFN_CONV, 2 * D_FF)), _resident((1, 2 * D_FF)),
                  _resident((D_FF, D_MODEL)), _resident((1, D_MODEL)), st],
        out_specs=[row, st],
        out_shape=[jax.ShapeDtypeStruct((b * t, D_MODEL), F32),
                   jax.ShapeDtypeStruct((b, FFN_CONV - 1, 2 * D_FF), F32)],
        scratch_shapes=[pltpu.VMEM((SUBLANES + tm, _FF_CHUNK), F32),
                        pltpu.VMEM((SUBLANES, 2 * D_FF), F32)],
        compiler_params=_params("parallel", "arbitrary"),
        name="ffn",
    )(x1, W["norm_ffn_pre"], W["w_up"], W["ffn_conv_w"], W["ffn_conv_b"], W["w_down"],
      W["norm_ffn_post"], conv0)


def _prep_weights(norm_mix_pre, w_in, b_in, fox_q_norm, fox_k_norm, mlstm_conv_w, mlstm_conv_b,
                  mlstm_head_norm, w_br_a, w_br_b, w_br_m, w_out, norm_mix_post, norm_ffn_pre,
                  w_up, ffn_conv_w, ffn_conv_b, w_down, norm_ffn_post):
    o_ff, o_mq, o_mi, o_mf, o_mo, o_cq, o_g = 1536, 1544, 3080, 3084, 3088, 3600, 4112

    def pack(a):
        pad = jnp.zeros(a.shape[:-1] + (LANES - FOX_HEADS - 2 * ML_HEADS,), a.dtype)
        small = jnp.concatenate([a[..., o_ff:o_mq], a[..., o_mi:o_mf], a[..., o_mf:o_mo], pad], -1)
        return jnp.concatenate([a[..., :o_ff], a[..., o_mq:o_mi], a[..., o_mo:o_g], small], -1)

    row = lambda v: v.reshape(1, -1).astype(F32)
    head = jnp.arange(FOX_W) // FOX_DIM
    return {
        "norm_mix_pre": row(norm_mix_pre),
        "w_in": pack(w_in).astype(BF16),
        "b_in": row(pack(b_in)),
        "w_gate": w_in[:, o_g:].astype(BF16),
        "b_gate": row(b_in[o_g:]),
        "fox_q_norm": row(jnp.tile(fox_q_norm, FOX_HEADS)),
        "fox_k_norm": row(jnp.tile(fox_k_norm, FOX_HEADS)),
        "head_avg": ((head[:, None] == head[None, :]).astype(F32) / FOX_DIM).astype(BF16),
        "mlstm_conv_w": mlstm_conv_w.astype(F32),
        "mlstm_conv_b": row(mlstm_conv_b),
        "mlstm_head_norm": row(mlstm_head_norm),
        "w_br_a": w_br_a.astype(BF16), "w_br_b": w_br_b.astype(BF16), "w_br_m": w_br_m.astype(BF16),
        "w_out": w_out.astype(BF16),
        "norm_mix_post": row(norm_mix_post),
        "norm_ffn_pre": row(norm_ffn_pre),
        "w_up": w_up.astype(BF16),
        "ffn_conv_w": ffn_conv_w.astype(F32),
        "ffn_conv_b": row(ffn_conv_b),
        "w_down": w_down.astype(BF16),
        "norm_ffn_post": row(norm_ffn_post),
    }


def _rows_layout(a, b, t, r):
    return jnp.transpose(a[:, :r].reshape(b, t, r), (0, 2, 1))


def _layer(x, fox_cache, ml_c0, ml_n0, ml_m0, ml_conv_prev, mem_k, mem_v, ffn_conv_prev, W):
    b, t, d = x.shape
    n = b * t
    x2 = x.reshape(n, d)
    q, kb, vb, fk, fv, mqk, mv, mo, cq, sm = _in_proj(x2, W)

    pairs = FOX_HEADS // 2
    f_rows = _rows_layout(sm, b, t, FOX_HEADS).reshape(b * FOX_HEADS, t)
    zeros_init = jnp.zeros((b * FOX_HEADS, LANES), F32)
    q3, fk3, fv3 = q.reshape(b, t, FOX_W), kb.reshape(b, t, FOX_W), vb.reshape(b, t, FOX_W)
    if fox_cache is None:
        c_new = _cumsum_rows(f_rows, zeros_init)
        a_out = _fox_attn_prompt(q3, fk3, fv3, c_new.reshape(b, pairs, 2, t))
    else:
        k_cache, v_cache, logf_cache = fox_cache
        plen = k_cache.shape[1]
        logf_rows = jnp.transpose(logf_cache.astype(F32), (0, 2, 1)).reshape(b * FOX_HEADS, plen)
        c_cache = _cumsum_rows(logf_rows, zeros_init)
        c_new = _cumsum_rows(f_rows, jnp.broadcast_to(c_cache[:, plen - 1:plen], (b * FOX_HEADS, LANES)))
        rows = b * plen * FOX_HEADS
        a_out = _fox_attn_cached(q3, fk3, fv3, c_new.reshape(b, FOX_HEADS, t),
                                 k_cache.astype(F32).reshape(rows, FOX_DIM),
                                 v_cache.astype(F32).reshape(rows, FOX_DIM), c_cache.reshape(b, FOX_HEADS, plen))

    smt = _rows_layout(sm, b, t, 16)
    b_out, ml_conv_new, c_t, n_t, m_t = _mlstm(
        mqk, mv, mo, sm, smt, W, ml_conv_prev.astype(F32), ml_c0.astype(F32),
        ml_n0.astype(F32).reshape(b, ML_HEADS, 1, ML_DIM),
        jnp.broadcast_to(ml_m0.astype(F32)[:, :, None, None], (b, ML_HEADS, 1, LANES)), b, t)

    mlen = mem_k.shape[1]
    m_out = _mem_attn(cq, mem_k.reshape(b, mlen, MEM_W), mem_v.reshape(b, mlen, MEM_W), b, t)

    x1 = _merge(x2, a_out.reshape(n, FOX_W), b_out, m_out, W)
    y, ffn_conv_new = _ffn(x1, ffn_conv_prev.astype(F32), W, b, t)

    states = (fk.reshape(b, t, FOX_HEADS, FOX_DIM), fv.reshape(b, t, FOX_HEADS, FOX_DIM),
              sm[:, :FOX_HEADS].reshape(b, t, FOX_HEADS), c_t, n_t.reshape(b, ML_HEADS, ML_DIM),
              m_t[:, :, 0, 0], ml_conv_new, ffn_conv_new)
    return y.reshape(b, t, d), states


def kernel(x_prompt, x_sample, cache_fox_k, cache_fox_v, cache_fox_logf, state_mlstm_c, state_mlstm_n, state_mlstm_m, state_mlstm_conv, cache_mem_k, cache_mem_v, state_ffn_conv, mem_prompt, norm_mix_pre, w_in, b_in, fox_q_norm, fox_k_norm, mlstm_conv_w, mlstm_conv_b, mlstm_head_norm, norm_mem, w_mem_kv, w_br_a, w_br_b, w_br_m, w_out, norm_mix_post, norm_ffn_pre, w_up, ffn_conv_w, ffn_conv_b, w_down, norm_ffn_post):
    depth = w_in.shape[0]
    hp, hs = x_prompt, x_sample
    b = x_prompt.shape[0]
    new_p = [[] for _ in range(10)]
    new_s = [[] for _ in range(8)]
    for l in range(depth):
        W = _prep_weights(norm_mix_pre[l], w_in[l], b_in[l], fox_q_norm[l], fox_k_norm[l],
                          mlstm_conv_w[l], mlstm_conv_b[l], mlstm_head_norm[l], w_br_a[l], w_br_b[l],
                          w_br_m[l], w_out[l], norm_mix_post[l], norm_ffn_pre[l], w_up[l],
                          ffn_conv_w[l], ffn_conv_b[l], w_down[l], norm_ffn_post[l])
        mlen = mem_prompt.shape[1]
        kv = _norm_matmul(mem_prompt.reshape(b * mlen, D_MODEL), norm_mem[l].reshape(1, -1).astype(F32),
                          w_mem_kv[l].astype(BF16))
        mem_k_p = kv[:, :MEM_W].reshape(b, mlen, MEM_HEADS, MEM_DIM)
        mem_v_p = kv[:, MEM_W:].reshape(b, mlen, MEM_HEADS, MEM_DIM)
        hp, st_p = _layer(
            hp, None,
            jnp.zeros((b, ML_HEADS, ML_DIM, ML_DIM), F32), jnp.zeros((b, ML_HEADS, ML_DIM), F32),
            jnp.zeros((b, ML_HEADS), F32), jnp.zeros((b, ML_CONV - 1, 2 * ML_W), F32),
            mem_k_p, mem_v_p, jnp.zeros((b, FFN_CONV - 1, 2 * D_FF), F32), W)
        hs, st_s = _layer(
            hs, (cache_fox_k[l], cache_fox_v[l], cache_fox_logf[l]),
            state_mlstm_c[l], state_mlstm_n[l], state_mlstm_m[l], state_mlstm_conv[l],
            cache_mem_k[l], cache_mem_v[l], state_ffn_conv[l], W)
        for acc, a in zip(new_p, st_p + (mem_k_p, mem_v_p)):
            acc.append(a)
        for acc, a in zip(new_s, st_s):
            acc.append(a)
    sp = [jnp.stack(a, axis=0) for a in new_p]
    ss = [jnp.stack(a, axis=0) for a in new_s]
    return (hp, hs, sp[0], sp[1], sp[2], sp[3], sp[4], sp[5], sp[6], sp[7], sp[8], sp[9],
            ss[0], ss[1], ss[2], ss[3], ss[4], ss[5], ss[6], ss[7])
```

```python
import functools

import jax
import jax.numpy as jnp
from jax import lax
from jax.experimental import pallas as pl
from jax.experimental.pallas import tpu as pltpu

F32 = jnp.float32
BF16 = jnp.bfloat16

D_MODEL = 1024
FOX_HEADS = 8
FOX_DIM = 64
ML_HEADS = 4
ML_DIM = 128
ML_CONV = 4
MEM_HEADS = 4
MEM_DIM = 128
D_FF = 2816
FFN_CONV = 3
N_BRANCH = 3
EPS = 1e-6
FOX_W = FOX_HEADS * FOX_DIM
ML_W = ML_HEADS * ML_DIM
MEM_W = MEM_HEADS * MEM_DIM

LANES = 128
SUBLANES = 8
NEG_BIG = -1e30
LOG2E = 1.4426950408889634
VMEM_LIMIT = 56 * 1024 * 1024

_C_FQ, _C_FK, _C_FV = 0, 512, 1024
_C_MQK, _C_MV, _C_MO, _C_CQ, _C_SM = 1536, 2560, 3072, 3584, 4096
_IN_COLS = 4224
_SM_FF, _SM_MI, _SM_MF = 0, 8, 12


def _params(*sem):
    return pltpu.CompilerParams(dimension_semantics=sem, vmem_limit_bytes=VMEM_LIMIT)


def _resident(shape):
    nd = len(shape)
    return pl.BlockSpec(shape, lambda *_: (0,) * nd, pipeline_mode=pl.Buffered(1))


def _rms(x, g):
    return x * lax.rsqrt(jnp.mean(x * x, axis=-1, keepdims=True) + EPS) * g


def _dot(a, b):
    return jnp.dot(a, b, preferred_element_type=F32)


def _dot_nt(a, b):
    return lax.dot_general(a, b, (((1,), (1,)), ((), ())), preferred_element_type=F32)


def _dot_tn(a, b):
    return lax.dot_general(a, b, (((0,), (0,)), ((), ())), preferred_element_type=F32)


def _log_sigmoid(z):
    return jnp.minimum(z, 0.0) - jnp.log1p(jnp.exp(-jnp.abs(z)))


def _sigmoid(z):
    return 1.0 / (1.0 + jnp.exp(-z))


def _in_proj_kernel(x_ref, g_ref, w_ref, b_ref, qn_ref, kn_ref, bd_ref,
                    q_ref, kb_ref, vb_ref, fk_ref, fv_ref, mqk_ref, mv_ref, mo_ref, cq_ref, sm_ref,
                    *, feature_major):
    xn = _rms(x_ref[...], g_ref[...]).astype(BF16)

    def proj(lo, hi):
        return _dot(xn, w_ref[:, lo:hi]) + b_ref[:, lo:hi]

    def head_norm(z, gn):
        ms = _dot((z * z).astype(BF16), bd_ref[...])
        return z * lax.rsqrt(ms + EPS) * gn

    def store_heads(dense_ref, state_ref, val):
        dense_ref[...] = val.astype(BF16)
        if feature_major:
            state_ref[0] = val.T
        else:
            for h in range(FOX_HEADS):
                state_ref[:, h, :] = val[:, h * FOX_DIM:(h + 1) * FOX_DIM]

    q_ref[...] = (head_norm(proj(_C_FQ, _C_FK), qn_ref[...]) * (FOX_DIM ** -0.5 * LOG2E)).astype(BF16)
    store_heads(kb_ref, fk_ref, head_norm(proj(_C_FK, _C_FV), kn_ref[...]))
    store_heads(vb_ref, fv_ref, proj(_C_FV, _C_MQK))
    mqk_ref[...] = proj(_C_MQK, _C_MV)
    mv_ref[...] = proj(_C_MV, _C_MO).astype(BF16)
    mo_ref[...] = proj(_C_MO, _C_CQ)
    cq_ref[...] = (proj(_C_CQ, _C_SM) * (MEM_DIM ** -0.5)).astype(BF16)
    z = proj(_C_SM, _IN_COLS)
    lane = lax.broadcasted_iota(jnp.int32, z.shape, 1)
    is_forget = (lane < _SM_MI) | ((lane >= _SM_MF) & (lane < _SM_MF + ML_HEADS))
    sm_ref[...] = jnp.where(is_forget, _log_sigmoid(z), z)


def _in_proj(x2, W, b, t):
    n = x2.shape[0]
    tm = min(512, n)
    row = lambda w: pl.BlockSpec((tm, w), lambda i: (i, 0))
    feature_major = t % tm == 0
    if feature_major:
        nt = t // tm
        heads = pl.BlockSpec((1, FOX_W, tm), lambda i: (i // nt, 0, i % nt))
        heads_shape = jax.ShapeDtypeStruct((b, FOX_W, t), F32)
    else:
        heads = pl.BlockSpec((tm, FOX_HEADS, FOX_DIM), lambda i: (i, 0, 0))
        heads_shape = jax.ShapeDtypeStruct((n, FOX_HEADS, FOX_DIM), F32)
    outs = [(FOX_W, BF16), (FOX_W, BF16), (FOX_W, BF16), None, None, (2 * ML_W, F32), (ML_W, BF16),
            (ML_W, F32), (MEM_W, BF16), (LANES, F32)]
    return pl.pallas_call(
        functools.partial(_in_proj_kernel, feature_major=feature_major),
        grid=(n // tm,),
        in_specs=[row(D_MODEL), _resident((1, D_MODEL)), _resident((D_MODEL, _IN_COLS)),
                  _resident((1, _IN_COLS)), _resident((1, FOX_W)), _resident((1, FOX_W)),
                  _resident((FOX_W, FOX_W))],
        out_specs=[heads if o is None else row(o[0]) for o in outs],
        out_shape=[heads_shape if o is None else jax.ShapeDtypeStruct((n, o[0]), o[1]) for o in outs],
        compiler_params=_params("parallel"),
        name="in_proj",
    )(x2, W["norm_mix_pre"], W["w_in"], W["b_in"], W["fox_q_norm"], W["fox_k_norm"], W["head_avg"])


def _norm_matmul_kernel(x_ref, g_ref, w_ref, o_ref):
    o_ref[...] = _dot(_rms(x_ref[...], g_ref[...]).astype(BF16), w_ref[...])


def _norm_matmul(x2, g, w):
    n, d = x2.shape
    cols = w.shape[1]
    tm = min(512, n)
    return pl.pallas_call(
        _norm_matmul_kernel,
        grid=(n // tm,),
        in_specs=[pl.BlockSpec((tm, d), lambda i: (i, 0)), _resident((1, d)), _resident((d, cols))],
        out_specs=pl.BlockSpec((tm, cols), lambda i: (i, 0)),
        out_shape=jax.ShapeDtypeStruct((n, cols), F32),
        compiler_params=_params("parallel"),
        name="mem_kv",
    )(x2, g, w)


def _cumsum_rows_kernel(f_ref, init_ref, o_ref, carry_ref):
    @pl.when(pl.program_id(0) == 0)
    def _():
        carry_ref[...] = init_ref[...]

    tb = f_ref.shape[-1]
    r = lax.broadcasted_iota(jnp.int32, (tb, tb), 0)
    c = lax.broadcasted_iota(jnp.int32, (tb, tb), 1)
    upper = (r <= c).astype(F32)
    cs = jnp.dot(f_ref[...] * LOG2E, upper, precision=lax.Precision.HIGHEST,
                 preferred_element_type=F32) + carry_ref[:, 0:1]
    o_ref[...] = cs
    carry_ref[...] = jnp.broadcast_to(cs[:, tb - 1:tb], carry_ref.shape)


def _cumsum_rows(f_rows, init):
    r, t = f_rows.shape
    tb = min(256, t)
    return pl.pallas_call(
        _cumsum_rows_kernel,
        grid=(t // tb,),
        in_specs=[pl.BlockSpec((r, tb), lambda j: (0, j)), _resident((r, LANES))],
        out_specs=pl.BlockSpec((r, tb), lambda j: (0, j)),
        out_shape=jax.ShapeDtypeStruct((r, t), F32),
        scratch_shapes=[pltpu.VMEM((r, LANES), F32)],
        compiler_params=_params("arbitrary"),
        name="cumsum_rows",
    )(f_rows, init)


def _stack_heads(q):
    lane = lax.broadcasted_iota(jnp.int32, q.shape, 1)
    zero = jnp.zeros_like(q)
    return jnp.concatenate([jnp.where(lane < FOX_DIM, q, zero), jnp.where(lane >= FOX_DIM, q, zero)], axis=0)


def _unstack_heads(o):
    tq = o.shape[0] // 2
    lane = lax.broadcasted_iota(jnp.int32, (tq, LANES), 1)
    return jnp.where(lane < FOX_DIM, o[:tq], o[tq:])


def _softmax_update(s, bias, mask, m_ref, l_ref):
    tq = s.shape[0] // len(bias)
    tk = s.shape[1]
    parts = [s[i * tq:(i + 1) * tq] + b for i, b in enumerate(bias)]
    s = jnp.concatenate(parts, axis=0) if len(parts) > 1 else parts[0]
    if mask is not None:
        s = jnp.where(mask, s, NEG_BIG)
    tiles = [s[:, i * LANES:(i + 1) * LANES] for i in range(tk // LANES)] if tk >= LANES else [s]
    m_cur = tiles[0]
    for tl in tiles[1:]:
        m_cur = jnp.maximum(m_cur, tl)
    m_prev = m_ref[...]
    m_new = jnp.maximum(m_prev, jnp.max(m_cur, axis=-1, keepdims=True))
    alpha = jnp.exp2(m_prev - m_new)
    if tk >= LANES:
        ps = [jnp.exp2(tl - m_new) for tl in tiles]
        l_ref[...] = alpha * l_ref[...] + sum(ps[1:], ps[0])
        p = jnp.concatenate(ps, axis=1) if len(ps) > 1 else ps[0]
    else:
        p = jnp.exp2(s - m_new[:, :tk])
        l_scaled = alpha * l_ref[...]
        l_ref[...] = l_scaled
        l_ref[:, :tk] = l_scaled[:, :tk] + p
    m_ref[...] = m_new
    return p.astype(BF16), alpha


def _init_stats(m_ref, l_ref, acc_ref):
    m_ref[...] = jnp.full(m_ref.shape, NEG_BIG, F32)
    l_ref[...] = jnp.zeros(l_ref.shape, F32)
    acc_ref[...] = jnp.zeros(acc_ref.shape, F32)


def _causal_stacked(tq):
    rr = lax.broadcasted_iota(jnp.int32, (2 * tq, tq), 0)
    cc = lax.broadcasted_iota(jnp.int32, (2 * tq, tq), 1)
    return cc <= jnp.where(rr >= tq, rr - tq, rr)


def _fox_prompt_kernel(q_ref, kb_ref, vb_ref, cn_ref, o_ref, q2_ref, m_ref, l_ref, acc_ref,
                       s_ref, p_ref, *, tq, nq):
    qi = pl.program_id(2)
    q2_ref[...] = _stack_heads(q_ref[0])
    _init_stats(m_ref, l_ref, acc_ref)
    t0 = pl.multiple_of(qi * tq, tq) if nq > 1 else 0
    anchor = [cn_ref[0, 0, hh:hh + 1, pl.ds(t0, tq)][:, 0:1] for hh in range(2)]

    def start(j):
        return pl.multiple_of(j * tq, tq) if nq > 1 else 0

    def scores(j):
        return _dot_nt(q2_ref[...], kb_ref[0, pl.ds(start(j), tq), :])

    def softmax(s, j, mask):
        bias = [anchor[hh] - cn_ref[0, 0, hh:hh + 1, pl.ds(start(j), tq)] for hh in range(2)]
        return _softmax_update(s, bias, mask, m_ref, l_ref)

    def weighted_values(p, j):
        return _dot(p, vb_ref[0, pl.ds(start(j), tq), :])

    def stage(j, slot, mask):
        s_ref[1 - slot] = scores(j + 1)
        pv_prev = weighted_values(p_ref[1 - slot], jnp.maximum(j - 1, 0))
        p_cur, alpha = softmax(s_ref[slot], j, mask)
        p_ref[slot] = p_cur
        acc_ref[...] = alpha * (acc_ref[...] + pv_prev)

    p_ref[...] = jnp.zeros(p_ref.shape, BF16)
    if nq > 1:
        odd = (qi & 1) == 1

        @pl.when(odd)
        def _():
            s_ref[1] = scores(0)
            stage(0, 1, None)

        @pl.when(jnp.logical_not(odd))
        def _():
            s_ref[0] = scores(0)

        j0 = qi & 1

        def body(i, carry):
            stage(j0 + 2 * i, 0, None)
            stage(j0 + 2 * i + 1, 1, None)
            return carry
        lax.fori_loop(0, lax.shift_right_logical(qi, 1), body, 0)
    else:
        s_ref[0] = scores(0)
    pv_prev = weighted_values(p_ref[1], jnp.maximum(qi - 1, 0))
    p_cur, alpha = softmax(s_ref[0], qi, _causal_stacked(tq))
    acc = alpha * (acc_ref[...] + pv_prev) + weighted_values(p_cur, qi)
    o = acc / jnp.sum(l_ref[...], axis=-1, keepdims=True)
    o_ref[0] = _unstack_heads(o).astype(BF16)


def _fox_attn_prompt(q, k_new, v_new, c_new):
    b, t, _ = q.shape
    tq = min(512, t)
    return pl.pallas_call(
        functools.partial(_fox_prompt_kernel, tq=tq, nq=t // tq),
        grid=(b, FOX_HEADS // 2, t // tq),
        in_specs=[pl.BlockSpec((1, tq, LANES), lambda i, p, j: (i, j, p)),
                  pl.BlockSpec((1, t, LANES), lambda i, p, j: (i, 0, p)),
                  pl.BlockSpec((1, t, LANES), lambda i, p, j: (i, 0, p)),
                  pl.BlockSpec((1, 1, 2, t), lambda i, p, j: (i, p, 0, 0))],
        out_specs=pl.BlockSpec((1, tq, LANES), lambda i, p, j: (i, j, p)),
        out_shape=jax.ShapeDtypeStruct((b, t, FOX_W), BF16),
        scratch_shapes=[pltpu.VMEM((2 * tq, LANES), BF16), pltpu.VMEM((2 * tq, LANES), F32),
                        pltpu.VMEM((2 * tq, LANES), F32), pltpu.VMEM((2 * tq, LANES), F32),
                        pltpu.VMEM((2, 2 * tq, tq), F32), pltpu.VMEM((2, 2 * tq, tq), BF16)],
        compiler_params=_params("parallel", "parallel", "arbitrary"),
        name="fox_attn_prompt",
    )(q, k_new, v_new, c_new)


def _fox_cached_kernel(q_ref, kn_ref, vn_ref, cn_ref, kc_ref, vc_ref, cc_ref, o_ref,
                       qh_ref, m_ref, l_ref, acc_ref, *, t, nblk, tkc):
    j = pl.program_id(1)
    head = lambda h: slice(h * FOX_DIM, (h + 1) * FOX_DIM)

    @pl.when(j == 0)
    def _():
        for h in range(FOX_HEADS):
            qh_ref[h] = q_ref[0, :, head(h)]
        _init_stats(m_ref, l_ref, acc_ref)

    def attend_all(scores, c_rows, mask, weighted_values):
        s_all = [scores(h) for h in range(FOX_HEADS)]
        stats = [_softmax_update(s_all[h], [cn_ref[0, h:h + 1, 0:1] - c_rows[0, h:h + 1, :]], mask,
                                 m_ref.at[h], l_ref.at[h]) for h in range(FOX_HEADS)]
        for h, (p, alpha) in enumerate(stats):
            acc_ref[h] = alpha[:, :FOX_DIM] * acc_ref[h] + weighted_values(h, p)

    attend_all(lambda h: _dot(qh_ref[h], kc_ref[head(h), :].astype(BF16)), cc_ref, None,
               lambda h, p: _dot_nt(p, vc_ref[head(h), :].astype(BF16)))

    @pl.when(j == nblk - 1)
    def _():
        rr = lax.broadcasted_iota(jnp.int32, (t, t), 0)
        cc = lax.broadcasted_iota(jnp.int32, (t, t), 1)
        attend_all(lambda h: _dot_nt(qh_ref[h], kn_ref[0, :, head(h)]), cn_ref, cc <= rr,
                   lambda h, p: _dot(p, vn_ref[0, :, head(h)]))
        for h in range(FOX_HEADS):
            o = acc_ref[h] / jnp.sum(l_ref[h], axis=-1, keepdims=True)
            o_ref[0, :, head(h)] = o.astype(BF16)


def _fox_attn_cached(q, k_new, v_new, c_new, k_cache, v_cache, c_cache):
    b, t, _ = q.shape
    plen = c_cache.shape[-1]
    tkc = min(512, plen)
    nblk = plen // tkc
    new = pl.BlockSpec((1, t, FOX_W), lambda i, j: (i, 0, 0))
    cache = pl.BlockSpec((FOX_W, tkc), lambda i, j: (i, j))
    return pl.pallas_call(
        functools.partial(_fox_cached_kernel, t=t, nblk=nblk, tkc=tkc),
        grid=(b, nblk),
        in_specs=[new, new, new, pl.BlockSpec((1, FOX_HEADS, t), lambda i, j: (i, 0, 0)),
                  cache, cache, pl.BlockSpec((1, FOX_HEADS, tkc), lambda i, j: (i, 0, j))],
        out_specs=new,
        out_shape=jax.ShapeDtypeStruct((b, t, FOX_W), BF16),
        scratch_shapes=[pltpu.VMEM((FOX_HEADS, t, FOX_DIM), BF16), pltpu.VMEM((FOX_HEADS, t, LANES), F32),
                        pltpu.VMEM((FOX_HEADS, t, LANES), F32), pltpu.VMEM((FOX_HEADS, t, FOX_DIM), F32)],
        compiler_params=_params("parallel", "arbitrary"),
        name="fox_attn_cached",
    )(q, k_new, v_new, c_new, k_cache, v_cache, c_cache)


_XP_HEAD = SUBLANES


def _mlstm_kernel(mqk_ref, mv_ref, mo_ref, sm_ref, smt_ref, cw_ref, cb_ref, hn_ref,
                  conv0_ref, c0_ref, n0_ref, m0_ref,
                  o_ref, conv_out_ref, c_out_ref, n_out_ref, m_out_ref,
                  xp_ref, c_ref, n_ref, m_ref, *, L):
    ci = pl.program_id(1)
    prev = ML_CONV - 1

    @pl.when(ci == 0)
    def _():
        xp_ref[_XP_HEAD - prev:_XP_HEAD, :] = conv0_ref[0]
        c_ref[...] = c0_ref[0]
        n_ref[...] = n0_ref[0]
        m_ref[...] = m0_ref[0]

    xp_ref[_XP_HEAD:_XP_HEAD + L, :] = mqk_ref[...]
    y = cb_ref[...] + sum(xp_ref[_XP_HEAD - prev + j:_XP_HEAD - prev + j + L, :] * cw_ref[j:j + 1, :]
                          for j in range(ML_CONV))
    qk = y * _sigmoid(y)
    tail = xp_ref[_XP_HEAD + L - prev:_XP_HEAD + L, :]
    conv_out_ref[0] = tail
    xp_ref[_XP_HEAD - prev:_XP_HEAD, :] = tail

    rr = lax.broadcasted_iota(jnp.int32, (L, L), 0)
    cc = lax.broadcasted_iota(jnp.int32, (L, L), 1)
    causal = cc <= rr
    sm = sm_ref[...]
    smt = smt_ref[0]
    hi = lax.Precision.HIGHEST
    bt_cols = jnp.dot(causal.astype(F32), sm, precision=hi, preferred_element_type=F32)
    bt_rows = jnp.dot(smt, (rr <= cc).astype(F32), precision=hi, preferred_element_type=F32)

    heads = range(ML_HEADS)
    sl = [slice(h * ML_DIM, (h + 1) * ML_DIM) for h in heads]
    q = [qk[:, sl[h]] * (ML_DIM ** -0.5) for h in heads]
    k = [qk[:, ML_W + h * ML_DIM:ML_W + (h + 1) * ML_DIM] for h in heads]
    qb = [a.astype(BF16) for a in q]
    kb = [a.astype(BF16) for a in k]
    c_p = [c_ref[h] for h in heads]
    n_p = [n_ref[h] for h in heads]
    m_p = [m_ref[h][:, 0:1] for h in heads]
    qk_scores = [_dot_nt(qb[h], kb[h]) for h in heads]
    q_c = [_dot_nt(qb[h], c_p[h].astype(BF16)) for h in heads]

    s_list, w_inter, den, m_t, wg, g_max, b_end = [], [], [], [], [], [], []
    for h in heads:
        it_col = sm[:, _SM_MI + h:_SM_MI + h + 1]
        it_row = smt[_SM_MI + h:_SM_MI + h + 1, :]
        bt_col = bt_cols[:, _SM_MF + h:_SM_MF + h + 1]
        bt_row = bt_rows[_SM_MF + h:_SM_MF + h + 1, :]
        b_end.append(bt_col[L - 1:L, :])
        log_w = bt_col - bt_row + it_row
        m_intra = jnp.max(jnp.where(causal, log_w, NEG_BIG), axis=-1, keepdims=True)
        log_inter = bt_col + m_p[h]
        m_t.append(jnp.maximum(log_inter, m_intra))
        dmat = jnp.where(causal, jnp.exp(log_w - m_t[h]), 0.0)
        s_list.append(qk_scores[h] * dmat)
        w_inter.append(jnp.exp(log_inter - m_t[h]))
        den.append(w_inter[h] * jnp.sum(q[h] * n_p[h], axis=-1, keepdims=True)
                   + jnp.sum(s_list[h], axis=-1, keepdims=True))
        g_col = b_end[h] - bt_col + it_col
        g_max.append(jnp.max(g_col, axis=0, keepdims=True))
        wg.append(jnp.exp(g_col - g_max[h]))

    s_v = [_dot(s_list[h].astype(BF16), mv_ref[:, sl[h]]) for h in heads]
    kv_blk = [_dot_tn((mv_ref[:, sl[h]].astype(F32) * wg[h]).astype(BF16), kb[h]) for h in heads]

    for h in heads:
        num = w_inter[h] * q_c[h] + s_v[h]
        hout = num / jnp.maximum(jnp.abs(den[h]), jnp.exp(-m_t[h]))
        hg = hout * _sigmoid(mo_ref[:, sl[h]])
        o_ref[:, sl[h]] = _rms(hg, hn_ref[:, sl[h]]).astype(BF16)

        k_blk = jnp.sum(k[h] * wg[h], axis=0, keepdims=True)
        m_new = jnp.maximum(b_end[h] + m_p[h], g_max[h])
        decay = jnp.exp(b_end[h] + m_p[h] - m_new)
        scale = jnp.exp(g_max[h] - m_new)
        c_new = decay * c_p[h] + scale * kv_blk[h]
        n_new = decay * n_p[h] + scale * k_blk
        c_ref[h] = c_new
        n_ref[h] = n_new
        m_ref[h] = jnp.broadcast_to(m_new, (1, LANES))
        c_out_ref[0, h] = c_new
        n_out_ref[0, h] = n_new
        m_out_ref[0, h] = jnp.broadcast_to(m_new, (1, LANES))


def _mlstm(mqk, mv, mo, sm, smt, W, conv0, c0, n0, m0, b, t):
    L = min(256, t)
    nc = t // L
    row = lambda w: pl.BlockSpec((L, w), lambda i, j: (i * nc + j, 0))
    st = lambda *shape: pl.BlockSpec((1,) + shape, lambda i, j: (i,) + (0,) * len(shape))
    return pl.pallas_call(
        functools.partial(_mlstm_kernel, L=L),
        grid=(b, nc),
        in_specs=[row(2 * ML_W), row(ML_W), row(ML_W), row(LANES),
                  pl.BlockSpec((1, 16, L), lambda i, j: (i, 0, j)),
                  _resident((ML_CONV, 2 * ML_W)), _resident((1, 2 * ML_W)), _resident((1, ML_W)),
                  st(ML_CONV - 1, 2 * ML_W), st(ML_HEADS, ML_DIM, ML_DIM),
                  st(ML_HEADS, 1, ML_DIM), st(ML_HEADS, 1, LANES)],
        out_specs=[row(ML_W), st(ML_CONV - 1, 2 * ML_W), st(ML_HEADS, ML_DIM, ML_DIM),
                   st(ML_HEADS, 1, ML_DIM), st(ML_HEADS, 1, LANES)],
        out_shape=[jax.ShapeDtypeStruct((b * t, ML_W), BF16),
                   jax.ShapeDtypeStruct((b, ML_CONV - 1, 2 * ML_W), F32),
                   jax.ShapeDtypeStruct((b, ML_HEADS, ML_DIM, ML_DIM), F32),
                   jax.ShapeDtypeStruct((b, ML_HEADS, 1, ML_DIM), F32),
                   jax.ShapeDtypeStruct((b, ML_HEADS, 1, LANES), F32)],
        scratch_shapes=[pltpu.VMEM((_XP_HEAD + L, 2 * ML_W), F32),
                        pltpu.VMEM((ML_HEADS, ML_DIM, ML_DIM), F32),
                        pltpu.VMEM((ML_HEADS, 1, ML_DIM), F32),
                        pltpu.VMEM((ML_HEADS, 1, LANES), F32)],
        compiler_params=_params("parallel", "arbitrary"),
        name="mlstm",
    )(mqk, mv, mo, sm, smt, W["mlstm_conv_w"], W["mlstm_conv_b"], W["mlstm_head_norm"],
      conv0, c0, n0, m0)


def _mem_attn_kernel(q_ref, k_ref, v_ref, o_ref):
    sl = [slice(h * MEM_DIM, (h + 1) * MEM_DIM) for h in range(MEM_HEADS)]
    s = [_dot_nt(q_ref[:, c], k_ref[0, :, c].astype(BF16)) for c in sl]
    p = [jnp.exp(a - jnp.max(a, axis=-1, keepdims=True)) for a in s]
    pv = [_dot(a.astype(BF16), v_ref[0, :, c].astype(BF16)) for a, c in zip(p, sl)]
    for a, o, c in zip(p, pv, sl):
        o_ref[:, c] = (o / jnp.sum(a, axis=-1, keepdims=True)).astype(BF16)


def _mem_attn(cq, mem_k, mem_v, b, t):
    tm = min(512, t)
    nt = t // tm
    mlen = mem_k.shape[1]
    kv = pl.BlockSpec((1, mlen, MEM_W), lambda i: (i // nt, 0, 0))
    return pl.pallas_call(
        _mem_attn_kernel,
        grid=(b * nt,),
        in_specs=[pl.BlockSpec((tm, MEM_W), lambda i: (i, 0)), kv, kv],
        out_specs=pl.BlockSpec((tm, MEM_W), lambda i: (i, 0)),
        out_shape=jax.ShapeDtypeStruct((b * t, MEM_W), BF16),
        compiler_params=_params("parallel"),
        name="mem_attn",
    )(cq, mem_k, mem_v)


def _merge_kernel(x_ref, a_ref, b_ref, m_ref, gpre_ref, wg_ref, bg_ref, wa_ref, wb_ref, wm_ref,
                  wo_ref, gpost_ref, o_ref):
    x = x_ref[...]
    xn = _rms(x, gpre_ref[...]).astype(BF16)
    merged = None
    for i, (br_ref, w_ref) in enumerate(((a_ref, wa_ref), (b_ref, wb_ref), (m_ref, wm_ref))):
        sl = slice(i * D_MODEL, (i + 1) * D_MODEL)
        gate = _sigmoid(_dot(xn, wg_ref[:, sl]) + bg_ref[:, sl])
        term = gate * _dot(br_ref[...], w_ref[...])
        merged = term if merged is None else merged + term
    o_ref[...] = x + _rms(_dot(merged.astype(BF16), wo_ref[...]), gpost_ref[...])


def _merge(x2, a, b, m, W):
    n = x2.shape[0]
    tm = min(512, n)
    row = lambda w: pl.BlockSpec((tm, w), lambda i: (i, 0))
    return pl.pallas_call(
        _merge_kernel,
        grid=(n // tm,),
        in_specs=[row(D_MODEL), row(FOX_W), row(ML_W), row(MEM_W), _resident((1, D_MODEL)),
                  _resident((D_MODEL, N_BRANCH * D_MODEL)), _resident((1, N_BRANCH * D_MODEL)),
                  _resident((FOX_W, D_MODEL)), _resident((ML_W, D_MODEL)), _resident((MEM_W, D_MODEL)),
                  _resident((D_MODEL, D_MODEL)), _resident((1, D_MODEL))],
        out_specs=row(D_MODEL),
        out_shape=jax.ShapeDtypeStruct((n, D_MODEL), F32),
        compiler_params=_params("parallel"),
        name="merge",
    )(x2, a, b, m, W["norm_mix_pre"], W["w_gate"], W["b_gate"], W["w_br_a"], W["w_br_b"],
      W["w_br_m"], W["w_out"], W["norm_mix_post"])


_FF_CHUNK = 512


def _ffn_kernel(x_ref, gpre_ref, wup_ref, cw_ref, cb_ref, wdn_ref, gpost_ref, conv0_ref,
                o_ref, conv_out_ref, buf_ref, carry_ref, *, tm):
    ti = pl.program_id(1)
    prev = FFN_CONV - 1

    @pl.when(ti == 0)
    def _():
        carry_ref[SUBLANES - prev:SUBLANES, :] = conv0_ref[0]

    x = x_ref[...]
    xn = _rms(x, gpre_ref[...]).astype(BF16)

    def conv_cols(lo, w):
        up = _dot(xn, wup_ref[:, lo:lo + w])
        buf_ref[SUBLANES - prev:SUBLANES, 0:w] = carry_ref[SUBLANES - prev:SUBLANES, lo:lo + w]
        buf_ref[SUBLANES:SUBLANES + tm, 0:w] = up
        tail = up[tm - prev:tm, :]
        carry_ref[SUBLANES - prev:SUBLANES, lo:lo + w] = tail
        conv_out_ref[0, :, lo:lo + w] = tail
        return cb_ref[:, lo:lo + w] + sum(
            buf_ref[SUBLANES - prev + j:SUBLANES - prev + j + tm, 0:w] * cw_ref[j:j + 1, lo:lo + w]
            for j in range(FFN_CONV))

    acc = None
    for lo in range(0, D_FF, _FF_CHUNK):
        w = min(_FF_CHUNK, D_FF - lo)
        ua = conv_cols(lo, w)
        ub = conv_cols(D_FF + lo, w)
        hid = (jax.nn.gelu(ua, approximate=True) * ub).astype(BF16)
        part = _dot(hid, wdn_ref[lo:lo + w, :])
        acc = part if acc is None else acc + part
    o_ref[...] = x + _rms(acc, gpost_ref[...])


def _ffn(x1, conv0, W, b, t):
    tm = min(512, t)
    nt = t // tm
    row = pl.BlockSpec((tm, D_MODEL), lambda i, j: (i * nt + j, 0))
    st = pl.BlockSpec((1, FFN_CONV - 1, 2 * D_FF), lambda i, j: (i, 0, 0))
    return pl.pallas_call(
        functools.partial(_ffn_kernel, tm=tm),
        grid=(b, nt),
        in_specs=[row, _resident((1, D_MODEL)), _resident((D_MODEL, 2 * D_FF)),
                  _resident((FFN_CONV, 2 * D_FF)), _resident((1, 2 * D_FF)),
                  _resident((D_FF, D_MODEL)), _resident((1, D_MODEL)), st],
        out_specs=[row, st],
        out_shape=[jax.ShapeDtypeStruct((b * t, D_MODEL), F32),
                   jax.ShapeDtypeStruct((b, FFN_CONV - 1, 2 * D_FF), F32)],
        scratch_shapes=[pltpu.VMEM((SUBLANES + tm, _FF_CHUNK), F32),
                        pltpu.VMEM((SUBLANES, 2 * D_FF), F32)],
        compiler_params=_params("parallel", "arbitrary"),
        name="ffn",
    )(x1, W["norm_ffn_pre"], W["w_up"], W["ffn_conv_w"], W["ffn_conv_b"], W["w_down"],
      W["norm_ffn_post"], conv0)


def _prep_weights(norm_mix_pre, w_in, b_in, fox_q_norm, fox_k_norm, mlstm_conv_w, mlstm_conv_b,
                  mlstm_head_norm, w_br_a, w_br_b, w_br_m, w_out, norm_mix_post, norm_ffn_pre,
                  w_up, ffn_conv_w, ffn_conv_b, w_down, norm_ffn_post):
    o_ff, o_mq, o_mi, o_mf, o_mo, o_cq, o_g = 1536, 1544, 3080, 3084, 3088, 3600, 4112

    def pack(a):
        pad = jnp.zeros(a.shape[:-1] + (LANES - FOX_HEADS - 2 * ML_HEADS,), a.dtype)
        small = jnp.concatenate([a[..., o_ff:o_mq], a[..., o_mi:o_mf], a[..., o_mf:o_mo], pad], -1)
        return jnp.concatenate([a[..., :o_ff], a[..., o_mq:o_mi], a[..., o_mo:o_g], small], -1)

    row = lambda v: v.reshape(1, -1).astype(F32)
    head = jnp.arange(FOX_W) // FOX_DIM
    return {
        "norm_mix_pre": row(norm_mix_pre),
        "w_in": pack(w_in).astype(BF16),
        "b_in": row(pack(b_in)),
        "w_gate": w_in[:, o_g:].astype(BF16),
        "b_gate": row(b_in[o_g:]),
        "fox_q_norm": row(jnp.tile(fox_q_norm, FOX_HEADS)),
        "fox_k_norm": row(jnp.tile(fox_k_norm, FOX_HEADS)),
        "head_avg": ((head[:, None] == head[None, :]).astype(F32) / FOX_DIM).astype(BF16),
        "mlstm_conv_w": mlstm_conv_w.astype(F32),
        "mlstm_conv_b": row(mlstm_conv_b),
        "mlstm_head_norm": row(mlstm_head_norm),
        "w_br_a": w_br_a.astype(BF16), "w_br_b": w_br_b.astype(BF16), "w_br_m": w_br_m.astype(BF16),
        "w_out": w_out.astype(BF16),
        "norm_mix_post": row(norm_mix_post),
        "norm_ffn_pre": row(norm_ffn_pre),
        "w_up": w_up.astype(BF16),
        "ffn_conv_w": ffn_conv_w.astype(F32),
        "ffn_conv_b": row(ffn_conv_b),
        "w_down": w_down.astype(BF16),
        "norm_ffn_post": row(norm_ffn_post),
    }


def _rows_layout(a, b, t, r):
    return jnp.transpose(a[:, :r].reshape(b, t, r), (0, 2, 1))


def _layer(x, fox_cache, ml_c0, ml_n0, ml_m0, ml_conv_prev, mem_k, mem_v, ffn_conv_prev, W):
    b, t, d = x.shape
    n = b * t
    x2 = x.reshape(n, d)
    q, kb, vb, fk, fv, mqk, mv, mo, cq, sm = _in_proj(x2, W, b, t)
    if fk.shape == (b, FOX_W, t):
        fk, fv = (jnp.transpose(a.reshape(b, FOX_HEADS, FOX_DIM, t), (0, 3, 1, 2)) for a in (fk, fv))

    pairs = FOX_HEADS // 2
    f_rows = _rows_layout(sm, b, t, FOX_HEADS).reshape(b * FOX_HEADS, t)
    zeros_init = jnp.zeros((b * FOX_HEADS, LANES), F32)
    q3, fk3, fv3 = q.reshape(b, t, FOX_W), kb.reshape(b, t, FOX_W), vb.reshape(b, t, FOX_W)
    if fox_cache is None:
        c_new = _cumsum_rows(f_rows, zeros_init)
        a_out = _fox_attn_prompt(q3, fk3, fv3, c_new.reshape(b, pairs, 2, t))
    else:
        k_cache, v_cache, logf_cache = fox_cache
        plen = k_cache.shape[1]
        logf_rows = jnp.transpose(logf_cache.astype(F32), (0, 2, 1)).reshape(b * FOX_HEADS, plen)
        c_cache = _cumsum_rows(logf_rows, zeros_init)
        c_new = _cumsum_rows(f_rows, jnp.broadcast_to(c_cache[:, plen - 1:plen], (b * FOX_HEADS, LANES)))
        feature_major = lambda a: jnp.transpose(a.astype(F32), (0, 2, 3, 1)).reshape(b * FOX_W, plen)
        a_out = _fox_attn_cached(q3, fk3, fv3, c_new.reshape(b, FOX_HEADS, t), feature_major(k_cache),
                                 feature_major(v_cache), c_cache.reshape(b, FOX_HEADS, plen))

    smt = _rows_layout(sm, b, t, 16)
    b_out, ml_conv_new, c_t, n_t, m_t = _mlstm(
        mqk, mv, mo, sm, smt, W, ml_conv_prev.astype(F32), ml_c0.astype(F32),
        ml_n0.astype(F32).reshape(b, ML_HEADS, 1, ML_DIM),
        jnp.broadcast_to(ml_m0.astype(F32)[:, :, None, None], (b, ML_HEADS, 1, LANES)), b, t)

    mlen = mem_k.shape[1]
    m_out = _mem_attn(cq, mem_k.reshape(b, mlen, MEM_W), mem_v.reshape(b, mlen, MEM_W), b, t)

    x1 = _merge(x2, a_out.reshape(n, FOX_W), b_out, m_out, W)
    y, ffn_conv_new = _ffn(x1, ffn_conv_prev.astype(F32), W, b, t)

    states = (fk.reshape(b, t, FOX_HEADS, FOX_DIM), fv.reshape(b, t, FOX_HEADS, FOX_DIM),
              sm[:, :FOX_HEADS].reshape(b, t, FOX_HEADS), c_t, n_t.reshape(b, ML_HEADS, ML_DIM),
              m_t[:, :, 0, 0], ml_conv_new, ffn_conv_new)
    return y.reshape(b, t, d), states


def kernel(x_prompt, x_sample, cache_fox_k, cache_fox_v, cache_fox_logf, state_mlstm_c, state_mlstm_n, state_mlstm_m, state_mlstm_conv, cache_mem_k, cache_mem_v, state_ffn_conv, mem_prompt, norm_mix_pre, w_in, b_in, fox_q_norm, fox_k_norm, mlstm_conv_w, mlstm_conv_b, mlstm_head_norm, norm_mem, w_mem_kv, w_br_a, w_br_b, w_br_m, w_out, norm_mix_post, norm_ffn_pre, w_up, ffn_conv_w, ffn_conv_b, w_down, norm_ffn_post):
    depth = w_in.shape[0]
    hp, hs = x_prompt, x_sample
    b = x_prompt.shape[0]
    new_p = [[] for _ in range(10)]
    new_s = [[] for _ in range(8)]
    for l in range(depth):
        W = _prep_weights(norm_mix_pre[l], w_in[l], b_in[l], fox_q_norm[l], fox_k_norm[l],
                          mlstm_conv_w[l], mlstm_conv_b[l], mlstm_head_norm[l], w_br_a[l], w_br_b[l],
                          w_br_m[l], w_out[l], norm_mix_post[l], norm_ffn_pre[l], w_up[l],
                          ffn_conv_w[l], ffn_conv_b[l], w_down[l], norm_ffn_post[l])
        mlen = mem_prompt.shape[1]
        kv = _norm_matmul(mem_prompt.reshape(b * mlen, D_MODEL), norm_mem[l].reshape(1, -1).astype(F32),
                          w_mem_kv[l].astype(BF16))
        mem_k_p = kv[:, :MEM_W].reshape(b, mlen, MEM_HEADS, MEM_DIM)
        mem_v_p = kv[:, MEM_W:].reshape(b, mlen, MEM_HEADS, MEM_DIM)
        hp, st_p = _layer(
            hp, None,
            jnp.zeros((b, ML_HEADS, ML_DIM, ML_DIM), F32), jnp.zeros((b, ML_HEADS, ML_DIM), F32),
            jnp.zeros((b, ML_HEADS), F32), jnp.zeros((b, ML_CONV - 1, 2 * ML_W), F32),
            mem_k_p, mem_v_p, jnp.zeros((b, FFN_CONV - 1, 2 * D_FF), F32), W)
        hs, st_s = _layer(
            hs, (cache_fox_k[l], cache_fox_v[l], cache_fox_logf[l]),
            state_mlstm_c[l], state_mlstm_n[l], state_mlstm_m[l], state_mlstm_conv[l],
            cache_mem_k[l], cache_mem_v[l], state_ffn_conv[l], W)
        for acc, a in zip(new_p, st_p + (mem_k_p, mem_v_p)):
            acc.append(a)
        for acc, a in zip(new_s, st_s):
            acc.append(a)
    sp = [jnp.stack(a, axis=0) for a in new_p]
    ss = [jnp.stack(a, axis=0) for a in new_s]
    return (hp, hs, sp[0], sp[1], sp[2], sp[3], sp[4], sp[5], sp[6], sp[7], sp[8], sp[9],
            ss[0], ss[1], ss[2], ss[3], ss[4], ss[5], ss[6], ss[7])
```

```python
import functools

import jax
import jax.numpy as jnp
from jax import lax
from jax.experimental import pallas as pl
from jax.experimental.pallas import tpu as pltpu

F32 = jnp.float32
BF16 = jnp.bfloat16

D_MODEL = 1024
FOX_HEADS = 8
FOX_DIM = 64
ML_HEADS = 4
ML_DIM = 128
ML_CONV = 4
MEM_HEADS = 4
MEM_DIM = 128
D_FF = 2816
FFN_CONV = 3
N_BRANCH = 3
EPS = 1e-6
FOX_W = FOX_HEADS * FOX_DIM
ML_W = ML_HEADS * ML_DIM
MEM_W = MEM_HEADS * MEM_DIM

LANES = 128
SUBLANES = 8
NEG_BIG = -1e30
LOG2E = 1.4426950408889634
VMEM_LIMIT = 56 * 1024 * 1024

_C_FQ, _C_FK, _C_FV = 0, 512, 1024
_C_MQK, _C_MV, _C_MO, _C_CQ, _C_SM = 1536, 2560, 3072, 3584, 4096
_IN_COLS = 4224
_IN_GROUPS = ("w_in_a", "w_in_b", "w_in_c", "w_in_s")
_SM_FF, _SM_MI, _SM_MF = 0, 8, 12


def _params(*sem):
    return pltpu.CompilerParams(dimension_semantics=sem, vmem_limit_bytes=VMEM_LIMIT)


def _resident(shape):
    nd = len(shape)
    return pl.BlockSpec(shape, lambda *_: (0,) * nd, pipeline_mode=pl.Buffered(1))


def _rms(x, g):
    return x * lax.rsqrt(jnp.mean(x * x, axis=-1, keepdims=True) + EPS) * g


def _dot(a, b):
    return jnp.dot(a, b, preferred_element_type=F32)


def _dot_nt(a, b):
    return lax.dot_general(a, b, (((1,), (1,)), ((), ())), preferred_element_type=F32)


def _dot_tn(a, b):
    return lax.dot_general(a, b, (((0,), (0,)), ((), ())), preferred_element_type=F32)


def _log_sigmoid(z):
    return jnp.minimum(z, 0.0) - jnp.log1p(jnp.exp(-jnp.abs(z)))


def _sigmoid(z):
    return 1.0 / (1.0 + jnp.exp(-z))


def _gelu_tanh(x):
    c = 0.7978845608028654
    hx = 0.5 * x
    return hx + hx * jnp.tanh(x * (c + (c * 0.044715) * (x * x)))


def _in_proj_kernel(x_ref, g_ref, wa_ref, wb_ref, wc_ref, ws_ref, b_ref, qn_ref, kn_ref, bd_ref,
                    q_ref, kb_ref, vb_ref, fk_ref, fv_ref, mqk_ref, mv_ref, mo_ref, cq_ref, sm_ref,
                    *, feature_major):
    xn = _rms(x_ref[...], g_ref[...]).astype(BF16)

    groups = ((_C_FQ, wa_ref), (_C_MQK, wb_ref), (_C_MO, wc_ref), (_C_SM, ws_ref))

    def proj(lo, hi):
        base, w_ref = [g for g in groups if g[0] <= lo][-1]
        return _dot(xn, w_ref[:, lo - base:hi - base]) + b_ref[:, lo:hi]

    def head_norm(z, gn):
        ms = _dot((z * z).astype(BF16), bd_ref[...])
        return z * lax.rsqrt(ms + EPS) * gn

    def store_heads(dense_ref, state_ref, val):
        dense_ref[...] = val.astype(BF16)
        if feature_major:
            state_ref[0] = val.T
        else:
            for h in range(FOX_HEADS):
                state_ref[:, h, :] = val[:, h * FOX_DIM:(h + 1) * FOX_DIM]

    q_ref[...] = (head_norm(proj(_C_FQ, _C_FK), qn_ref[...]) * (FOX_DIM ** -0.5 * LOG2E)).astype(BF16)
    store_heads(kb_ref, fk_ref, head_norm(proj(_C_FK, _C_FV), kn_ref[...]))
    store_heads(vb_ref, fv_ref, proj(_C_FV, _C_MQK))
    mqk_ref[...] = proj(_C_MQK, _C_MV)
    mv_ref[...] = proj(_C_MV, _C_MO).astype(BF16)
    mo_ref[...] = proj(_C_MO, _C_CQ)
    cq_ref[...] = (proj(_C_CQ, _C_SM) * (MEM_DIM ** -0.5)).astype(BF16)
    z = proj(_C_SM, _IN_COLS)
    lane = lax.broadcasted_iota(jnp.int32, z.shape, 1)
    is_forget = (lane < _SM_MI) | ((lane >= _SM_MF) & (lane < _SM_MF + ML_HEADS))
    sm_ref[...] = jnp.where(is_forget, _log_sigmoid(z), z)


def _in_proj(x2, W, b, t):
    n = x2.shape[0]
    tm = min(512, n)
    row = lambda w: pl.BlockSpec((tm, w), lambda i: (i, 0))
    feature_major = t % tm == 0
    if feature_major:
        nt = t // tm
        heads = pl.BlockSpec((1, FOX_W, tm), lambda i: (i // nt, 0, i % nt))
        heads_shape = jax.ShapeDtypeStruct((b, FOX_W, t), F32)
    else:
        heads = pl.BlockSpec((tm, FOX_HEADS, FOX_DIM), lambda i: (i, 0, 0))
        heads_shape = jax.ShapeDtypeStruct((n, FOX_HEADS, FOX_DIM), F32)
    outs = [(FOX_W, BF16), (FOX_W, BF16), (FOX_W, BF16), None, None, (2 * ML_W, F32), (ML_W, BF16),
            (ML_W, F32), (MEM_W, BF16), (LANES, F32)]
    return pl.pallas_call(
        functools.partial(_in_proj_kernel, feature_major=feature_major),
        grid=(n // tm,),
        in_specs=[row(D_MODEL), _resident((1, D_MODEL))] + [_resident(W[k].shape) for k in _IN_GROUPS]
                 + [_resident((1, _IN_COLS)), _resident((1, FOX_W)), _resident((1, FOX_W)),
                    _resident((FOX_W, FOX_W))],
        out_specs=[heads if o is None else row(o[0]) for o in outs],
        out_shape=[heads_shape if o is None else jax.ShapeDtypeStruct((n, o[0]), o[1]) for o in outs],
        compiler_params=_params("parallel"),
        name="in_proj",
    )(x2, W["norm_mix_pre"], *[W[k] for k in _IN_GROUPS], W["b_in"], W["fox_q_norm"], W["fox_k_norm"],
      W["head_avg"])


def _norm_matmul_kernel(x_ref, g_ref, w_ref, o_ref):
    o_ref[...] = _dot(_rms(x_ref[...], g_ref[...]).astype(BF16), w_ref[...])


def _norm_matmul(x2, g, w):
    n, d = x2.shape
    cols = w.shape[1]
    tm = min(512, n)
    return pl.pallas_call(
        _norm_matmul_kernel,
        grid=(n // tm,),
        in_specs=[pl.BlockSpec((tm, d), lambda i: (i, 0)), _resident((1, d)), _resident((d, cols))],
        out_specs=pl.BlockSpec((tm, cols), lambda i: (i, 0)),
        out_shape=jax.ShapeDtypeStruct((n, cols), F32),
        compiler_params=_params("parallel"),
        name="mem_kv",
    )(x2, g, w)


def _cumsum_rows_kernel(f_ref, init_ref, o_ref, carry_ref):
    @pl.when(pl.program_id(0) == 0)
    def _():
        carry_ref[...] = init_ref[...]

    tb = f_ref.shape[-1]
    r = lax.broadcasted_iota(jnp.int32, (tb, tb), 0)
    c = lax.broadcasted_iota(jnp.int32, (tb, tb), 1)
    upper = (r <= c).astype(F32)
    cs = jnp.dot(f_ref[...] * LOG2E, upper, precision=lax.Precision.HIGHEST,
                 preferred_element_type=F32) + carry_ref[:, 0:1]
    o_ref[...] = cs
    carry_ref[...] = jnp.broadcast_to(cs[:, tb - 1:tb], carry_ref.shape)


def _cumsum_rows(f_rows, init):
    r, t = f_rows.shape
    tb = min(256, t)
    return pl.pallas_call(
        _cumsum_rows_kernel,
        grid=(t // tb,),
        in_specs=[pl.BlockSpec((r, tb), lambda j: (0, j)), _resident((r, LANES))],
        out_specs=pl.BlockSpec((r, tb), lambda j: (0, j)),
        out_shape=jax.ShapeDtypeStruct((r, t), F32),
        scratch_shapes=[pltpu.VMEM((r, LANES), F32)],
        compiler_params=_params("arbitrary"),
        name="cumsum_rows",
    )(f_rows, init)


def _stack_heads(q):
    lane = lax.broadcasted_iota(jnp.int32, q.shape, 1)
    zero = jnp.zeros_like(q)
    return jnp.concatenate([jnp.where(lane < FOX_DIM, q, zero), jnp.where(lane >= FOX_DIM, q, zero)], axis=0)


def _unstack_heads(o):
    tq = o.shape[0] // 2
    lane = lax.broadcasted_iota(jnp.int32, (tq, LANES), 1)
    return jnp.where(lane < FOX_DIM, o[:tq], o[tq:])


def _softmax_update(s, bias, mask, m_ref, l_ref):
    tq = s.shape[0] // len(bias)
    tk = s.shape[1]
    parts = [s[i * tq:(i + 1) * tq] + b for i, b in enumerate(bias)]
    s = jnp.concatenate(parts, axis=0) if len(parts) > 1 else parts[0]
    if mask is not None:
        s = jnp.where(mask, s, NEG_BIG)
    tiles = [s[:, i * LANES:(i + 1) * LANES] for i in range(tk // LANES)] if tk >= LANES else [s]
    m_cur = tiles[0]
    for tl in tiles[1:]:
        m_cur = jnp.maximum(m_cur, tl)
    m_prev = m_ref[...]
    m_new = jnp.maximum(m_prev, jnp.max(m_cur, axis=-1, keepdims=True))
    alpha = jnp.exp2(m_prev - m_new)
    if tk >= LANES:
        ps = [jnp.exp2(tl - m_new) for tl in tiles]
        l_ref[...] = alpha * l_ref[...] + sum(ps[1:], ps[0])
        p = jnp.concatenate(ps, axis=1) if len(ps) > 1 else ps[0]
    else:
        p = jnp.exp2(s - m_new[:, :tk])
        l_scaled = alpha * l_ref[...]
        l_ref[...] = l_scaled
        l_ref[:, :tk] = l_scaled[:, :tk] + p
    m_ref[...] = m_new
    return p.astype(BF16), alpha


def _init_stats(m_ref, l_ref, acc_ref):
    m_ref[...] = jnp.full(m_ref.shape, NEG_BIG, F32)
    l_ref[...] = jnp.zeros(l_ref.shape, F32)
    acc_ref[...] = jnp.zeros(acc_ref.shape, F32)


def _causal_stacked(tq):
    rr = lax.broadcasted_iota(jnp.int32, (2 * tq, tq), 0)
    cc = lax.broadcasted_iota(jnp.int32, (2 * tq, tq), 1)
    return cc <= jnp.where(rr >= tq, rr - tq, rr)


def _fox_prompt_kernel(q_ref, kb_ref, vb_ref, cn_ref, o_ref, q2_ref, m_ref, l_ref, acc_ref,
                       s_ref, p_ref, *, tq, nq):
    qi = pl.program_id(2)
    q2_ref[...] = _stack_heads(q_ref[0])
    _init_stats(m_ref, l_ref, acc_ref)
    t0 = pl.multiple_of(qi * tq, tq) if nq > 1 else 0
    anchor = [cn_ref[0, 0, hh:hh + 1, pl.ds(t0, tq)][:, 0:1] for hh in range(2)]

    def start(j):
        return pl.multiple_of(j * tq, tq) if nq > 1 else 0

    def scores(j):
        return _dot_nt(q2_ref[...], kb_ref[0, pl.ds(start(j), tq), :])

    def softmax(s, j, mask):
        bias = [anchor[hh] - cn_ref[0, 0, hh:hh + 1, pl.ds(start(j), tq)] for hh in range(2)]
        return _softmax_update(s, bias, mask, m_ref, l_ref)

    def weighted_values(p, j):
        return _dot(p, vb_ref[0, pl.ds(start(j), tq), :])

    def stage(j, slot, mask):
        s_ref[1 - slot] = scores(j + 1)
        pv_prev = weighted_values(p_ref[1 - slot], jnp.maximum(j - 1, 0))
        p_cur, alpha = softmax(s_ref[slot], j, mask)
        p_ref[slot] = p_cur
        acc_ref[...] = alpha * (acc_ref[...] + pv_prev)

    p_ref[...] = jnp.zeros(p_ref.shape, BF16)
    if nq > 1:
        odd = (qi & 1) == 1

        @pl.when(odd)
        def _():
            s_ref[1] = scores(0)
            stage(0, 1, None)

        @pl.when(jnp.logical_not(odd))
        def _():
            s_ref[0] = scores(0)

        j0 = qi & 1

        def body(i, carry):
            stage(j0 + 2 * i, 0, None)
            stage(j0 + 2 * i + 1, 1, None)
            return carry
        lax.fori_loop(0, lax.shift_right_logical(qi, 1), body, 0)
    else:
        s_ref[0] = scores(0)
    pv_prev = weighted_values(p_ref[1], jnp.maximum(qi - 1, 0))
    p_cur, alpha = softmax(s_ref[0], qi, _causal_stacked(tq))
    acc = alpha * (acc_ref[...] + pv_prev) + weighted_values(p_cur, qi)
    o = acc / jnp.sum(l_ref[...], axis=-1, keepdims=True)
    o_ref[0] = _unstack_heads(o).astype(BF16)


def _fox_attn_prompt(q, k_new, v_new, c_new):
    b, t, _ = q.shape
    tq = min(512, t)
    return pl.pallas_call(
        functools.partial(_fox_prompt_kernel, tq=tq, nq=t // tq),
        grid=(b, FOX_HEADS // 2, t // tq),
        in_specs=[pl.BlockSpec((1, tq, LANES), lambda i, p, j: (i, j, p)),
                  pl.BlockSpec((1, t, LANES), lambda i, p, j: (i, 0, p)),
                  pl.BlockSpec((1, t, LANES), lambda i, p, j: (i, 0, p)),
                  pl.BlockSpec((1, 1, 2, t), lambda i, p, j: (i, p, 0, 0))],
        out_specs=pl.BlockSpec((1, tq, LANES), lambda i, p, j: (i, j, p)),
        out_shape=jax.ShapeDtypeStruct((b, t, FOX_W), BF16),
        scratch_shapes=[pltpu.VMEM((2 * tq, LANES), BF16), pltpu.VMEM((2 * tq, LANES), F32),
                        pltpu.VMEM((2 * tq, LANES), F32), pltpu.VMEM((2 * tq, LANES), F32),
                        pltpu.VMEM((2, 2 * tq, tq), F32), pltpu.VMEM((2, 2 * tq, tq), BF16)],
        compiler_params=_params("parallel", "parallel", "arbitrary"),
        name="fox_attn_prompt",
    )(q, k_new, v_new, c_new)


def _fox_cached_kernel(q_ref, kn_ref, vn_ref, cn_ref, kc_ref, vc_ref, cc_ref, o_ref,
                       qh_ref, m_ref, l_ref, acc_ref, *, t, nblk, tkc):
    j = pl.program_id(1)
    head = lambda h: slice(h * FOX_DIM, (h + 1) * FOX_DIM)

    @pl.when(j == 0)
    def _():
        for h in range(FOX_HEADS):
            qh_ref[h] = q_ref[0, :, head(h)]
        _init_stats(m_ref, l_ref, acc_ref)

    def attend_all(scores, c_rows, mask, weighted_values):
        s_all = [scores(h) for h in range(FOX_HEADS)]
        stats = [_softmax_update(s_all[h], [cn_ref[0, h:h + 1, 0:1] - c_rows[0, h:h + 1, :]], mask,
                                 m_ref.at[h], l_ref.at[h]) for h in range(FOX_HEADS)]
        for h, (p, alpha) in enumerate(stats):
            acc_ref[h] = alpha[:, :FOX_DIM] * acc_ref[h] + weighted_values(h, p)

    attend_all(lambda h: _dot(qh_ref[h], kc_ref[head(h), :].astype(BF16)), cc_ref, None,
               lambda h, p: _dot_nt(p, vc_ref[head(h), :].astype(BF16)))

    @pl.when(j == nblk - 1)
    def _():
        rr = lax.broadcasted_iota(jnp.int32, (t, t), 0)
        cc = lax.broadcasted_iota(jnp.int32, (t, t), 1)
        attend_all(lambda h: _dot_nt(qh_ref[h], kn_ref[0, :, head(h)]), cn_ref, cc <= rr,
                   lambda h, p: _dot(p, vn_ref[0, :, head(h)]))
        for h in range(FOX_HEADS):
            o = acc_ref[h] / jnp.sum(l_ref[h], axis=-1, keepdims=True)
            o_ref[0, :, head(h)] = o.astype(BF16)


def _fox_attn_cached(q, k_new, v_new, c_new, k_cache, v_cache, c_cache):
    b, t, _ = q.shape
    plen = c_cache.shape[-1]
    tkc = min(512, plen)
    nblk = plen // tkc
    new = pl.BlockSpec((1, t, FOX_W), lambda i, j: (i, 0, 0))
    cache = pl.BlockSpec((FOX_W, tkc), lambda i, j: (i, j))
    return pl.pallas_call(
        functools.partial(_fox_cached_kernel, t=t, nblk=nblk, tkc=tkc),
        grid=(b, nblk),
        in_specs=[new, new, new, pl.BlockSpec((1, FOX_HEADS, t), lambda i, j: (i, 0, 0)),
                  cache, cache, pl.BlockSpec((1, FOX_HEADS, tkc), lambda i, j: (i, 0, j))],
        out_specs=new,
        out_shape=jax.ShapeDtypeStruct((b, t, FOX_W), BF16),
        scratch_shapes=[pltpu.VMEM((FOX_HEADS, t, FOX_DIM), BF16), pltpu.VMEM((FOX_HEADS, t, LANES), F32),
                        pltpu.VMEM((FOX_HEADS, t, LANES), F32), pltpu.VMEM((FOX_HEADS, t, FOX_DIM), F32)],
        compiler_params=_params("parallel", "arbitrary"),
        name="fox_attn_cached",
    )(q, k_new, v_new, c_new, k_cache, v_cache, c_cache)


_XP_HEAD = SUBLANES


def _mlstm_kernel(mqk_ref, mv_ref, mo_ref, sm_ref, smt_ref, cw_ref, cb_ref, hn_ref,
                  conv0_ref, c0_ref, n0_ref, m0_ref,
                  o_ref, conv_out_ref, c_out_ref, n_out_ref, m_out_ref,
                  xp_ref, c_ref, n_ref, m_ref, *, L):
    ci = pl.program_id(1)
    prev = ML_CONV - 1

    @pl.when(ci == 0)
    def _():
        xp_ref[_XP_HEAD - prev:_XP_HEAD, :] = conv0_ref[0]
        c_ref[...] = c0_ref[0]
        n_ref[...] = n0_ref[0]
        m_ref[...] = m0_ref[0]

    xp_ref[_XP_HEAD:_XP_HEAD + L, :] = mqk_ref[...]
    y = cb_ref[...] + sum(xp_ref[_XP_HEAD - prev + j:_XP_HEAD - prev + j + L, :] * cw_ref[j:j + 1, :]
                          for j in range(ML_CONV))
    qk = y * _sigmoid(y)
    tail = xp_ref[_XP_HEAD + L - prev:_XP_HEAD + L, :]
    conv_out_ref[0] = tail
    xp_ref[_XP_HEAD - prev:_XP_HEAD, :] = tail

    rr = lax.broadcasted_iota(jnp.int32, (L, L), 0)
    cc = lax.broadcasted_iota(jnp.int32, (L, L), 1)
    causal = cc <= rr
    sm = sm_ref[...]
    smt = smt_ref[0]
    hi = lax.Precision.HIGHEST
    bt_cols = jnp.dot(causal.astype(F32), sm, precision=hi, preferred_element_type=F32)
    bt_rows = jnp.dot(smt, (rr <= cc).astype(F32), precision=hi, preferred_element_type=F32)

    heads = range(ML_HEADS)
    sl = [slice(h * ML_DIM, (h + 1) * ML_DIM) for h in heads]
    q = [qk[:, sl[h]] * (ML_DIM ** -0.5) for h in heads]
    k = [qk[:, ML_W + h * ML_DIM:ML_W + (h + 1) * ML_DIM] for h in heads]
    qb = [a.astype(BF16) for a in q]
    kb = [a.astype(BF16) for a in k]
    c_p = [c_ref[h] for h in heads]
    n_p = [n_ref[h] for h in heads]
    m_p = [m_ref[h][:, 0:1] for h in heads]
    qk_scores = [_dot_nt(qb[h], kb[h]) for h in heads]
    q_c = [_dot_nt(qb[h], c_p[h].astype(BF16)) for h in heads]

    s_list, w_inter, den, m_t, wg, g_max, b_end = [], [], [], [], [], [], []
    for h in heads:
        it_col = sm[:, _SM_MI + h:_SM_MI + h + 1]
        it_row = smt[_SM_MI + h:_SM_MI + h + 1, :]
        bt_col = bt_cols[:, _SM_MF + h:_SM_MF + h + 1]
        bt_row = bt_rows[_SM_MF + h:_SM_MF + h + 1, :]
        b_end.append(bt_col[L - 1:L, :])
        log_w = bt_col - bt_row + it_row
        m_intra = jnp.max(jnp.where(causal, log_w, NEG_BIG), axis=-1, keepdims=True)
        log_inter = bt_col + m_p[h]
        m_t.append(jnp.maximum(log_inter, m_intra))
        dmat = jnp.where(causal, jnp.exp(log_w - m_t[h]), 0.0)
        s_list.append(qk_scores[h] * dmat)
        w_inter.append(jnp.exp(log_inter - m_t[h]))
        den.append(w_inter[h] * jnp.sum(q[h] * n_p[h], axis=-1, keepdims=True)
                   + jnp.sum(s_list[h], axis=-1, keepdims=True))
        g_col = b_end[h] - bt_col + it_col
        g_max.append(jnp.max(g_col, axis=0, keepdims=True))
        wg.append(jnp.exp(g_col - g_max[h]))

    s_v = [_dot(s_list[h].astype(BF16), mv_ref[:, sl[h]]) for h in heads]
    kv_blk = [_dot_tn((mv_ref[:, sl[h]].astype(F32) * wg[h]).astype(BF16), kb[h]) for h in heads]

    for h in heads:
        num = w_inter[h] * q_c[h] + s_v[h]
        hout = num / jnp.maximum(jnp.abs(den[h]), jnp.exp(-m_t[h]))
        hg = hout * _sigmoid(mo_ref[:, sl[h]])
        o_ref[:, sl[h]] = _rms(hg, hn_ref[:, sl[h]]).astype(BF16)

        k_blk = jnp.sum(k[h] * wg[h], axis=0, keepdims=True)
        m_new = jnp.maximum(b_end[h] + m_p[h], g_max[h])
        decay = jnp.exp(b_end[h] + m_p[h] - m_new)
        scale = jnp.exp(g_max[h] - m_new)
        c_new = decay * c_p[h] + scale * kv_blk[h]
        n_new = decay * n_p[h] + scale * k_blk
        c_ref[h] = c_new
        n_ref[h] = n_new
        m_ref[h] = jnp.broadcast_to(m_new, (1, LANES))
        c_out_ref[0, h] = c_new
        n_out_ref[0, h] = n_new
        m_out_ref[0, h] = jnp.broadcast_to(m_new, (1, LANES))


def _mlstm(mqk, mv, mo, sm, smt, W, conv0, c0, n0, m0, b, t):
    L = min(256, t)
    nc = t // L
    row = lambda w: pl.BlockSpec((L, w), lambda i, j: (i * nc + j, 0))
    st = lambda *shape: pl.BlockSpec((1,) + shape, lambda i, j: (i,) + (0,) * len(shape))
    return pl.pallas_call(
        functools.partial(_mlstm_kernel, L=L),
        grid=(b, nc),
        in_specs=[row(2 * ML_W), row(ML_W), row(ML_W), row(LANES),
                  pl.BlockSpec((1, 16, L), lambda i, j: (i, 0, j)),
                  _resident((ML_CONV, 2 * ML_W)), _resident((1, 2 * ML_W)), _resident((1, ML_W)),
                  st(ML_CONV - 1, 2 * ML_W), st(ML_HEADS, ML_DIM, ML_DIM),
                  st(ML_HEADS, 1, ML_DIM), st(ML_HEADS, 1, LANES)],
        out_specs=[row(ML_W), st(ML_CONV - 1, 2 * ML_W), st(ML_HEADS, ML_DIM, ML_DIM),
                   st(ML_HEADS, 1, ML_DIM), st(ML_HEADS, 1, LANES)],
        out_shape=[jax.ShapeDtypeStruct((b * t, ML_W), BF16),
                   jax.ShapeDtypeStruct((b, ML_CONV - 1, 2 * ML_W), F32),
                   jax.ShapeDtypeStruct((b, ML_HEADS, ML_DIM, ML_DIM), F32),
                   jax.ShapeDtypeStruct((b, ML_HEADS, 1, ML_DIM), F32),
                   jax.ShapeDtypeStruct((b, ML_HEADS, 1, LANES), F32)],
        scratch_shapes=[pltpu.VMEM((_XP_HEAD + L, 2 * ML_W), F32),
                        pltpu.VMEM((ML_HEADS, ML_DIM, ML_DIM), F32),
                        pltpu.VMEM((ML_HEADS, 1, ML_DIM), F32),
                        pltpu.VMEM((ML_HEADS, 1, LANES), F32)],
        compiler_params=_params("parallel", "arbitrary"),
        name="mlstm",
    )(mqk, mv, mo, sm, smt, W["mlstm_conv_w"], W["mlstm_conv_b"], W["mlstm_head_norm"],
      conv0, c0, n0, m0)


def _mem_attn_kernel(q_ref, k_ref, v_ref, o_ref):
    sl = [slice(h * MEM_DIM, (h + 1) * MEM_DIM) for h in range(MEM_HEADS)]
    s = [_dot_nt(q_ref[:, c], k_ref[0, :, c].astype(BF16)) for c in sl]
    p = [jnp.exp(a - jnp.max(a, axis=-1, keepdims=True)) for a in s]
    pv = [_dot(a.astype(BF16), v_ref[0, :, c].astype(BF16)) for a, c in zip(p, sl)]
    for a, o, c in zip(p, pv, sl):
        o_ref[:, c] = (o / jnp.sum(a, axis=-1, keepdims=True)).astype(BF16)


def _mem_attn(cq, mem_k, mem_v, b, t):
    tm = min(512, t)
    nt = t // tm
    mlen = mem_k.shape[1]
    kv = pl.BlockSpec((1, mlen, MEM_W), lambda i: (i // nt, 0, 0))
    return pl.pallas_call(
        _mem_attn_kernel,
        grid=(b * nt,),
        in_specs=[pl.BlockSpec((tm, MEM_W), lambda i: (i, 0)), kv, kv],
        out_specs=pl.BlockSpec((tm, MEM_W), lambda i: (i, 0)),
        out_shape=jax.ShapeDtypeStruct((b * t, MEM_W), BF16),
        compiler_params=_params("parallel"),
        name="mem_attn",
    )(cq, mem_k, mem_v)


def _merge_kernel(x_ref, a_ref, b_ref, m_ref, gpre_ref, wg_ref, bg_ref, wa_ref, wb_ref, wm_ref,
                  wo_ref, gpost_ref, o_ref):
    x = x_ref[...]
    xn = _rms(x, gpre_ref[...]).astype(BF16)
    merged = None
    for i, (br_ref, w_ref) in enumerate(((a_ref, wa_ref), (b_ref, wb_ref), (m_ref, wm_ref))):
        sl = slice(i * D_MODEL, (i + 1) * D_MODEL)
        gate = _sigmoid(_dot(xn, wg_ref[:, sl]) + bg_ref[:, sl])
        term = gate * _dot(br_ref[...], w_ref[...])
        merged = term if merged is None else merged + term
    o_ref[...] = x + _rms(_dot(merged.astype(BF16), wo_ref[...]), gpost_ref[...])


def _merge(x2, a, b, m, W):
    n = x2.shape[0]
    tm = min(512, n)
    row = lambda w: pl.BlockSpec((tm, w), lambda i: (i, 0))
    return pl.pallas_call(
        _merge_kernel,
        grid=(n // tm,),
        in_specs=[row(D_MODEL), row(FOX_W), row(ML_W), row(MEM_W), _resident((1, D_MODEL)),
                  _resident((D_MODEL, N_BRANCH * D_MODEL)), _resident((1, N_BRANCH * D_MODEL)),
                  _resident((FOX_W, D_MODEL)), _resident((ML_W, D_MODEL)), _resident((MEM_W, D_MODEL)),
                  _resident((D_MODEL, D_MODEL)), _resident((1, D_MODEL))],
        out_specs=row(D_MODEL),
        out_shape=jax.ShapeDtypeStruct((n, D_MODEL), F32),
        compiler_params=_params("parallel"),
        name="merge",
    )(x2, a, b, m, W["norm_mix_pre"], W["w_gate"], W["b_gate"], W["w_br_a"], W["w_br_b"],
      W["w_br_m"], W["w_out"], W["norm_mix_post"])


_FF_CHUNK = 512


def _ffn_kernel(x_ref, gpre_ref, wup_ref, cw_ref, cb_ref, wdn_ref, gpost_ref, conv0_ref,
                o_ref, conv_out_ref, buf_ref, carry_ref, hid_ref, *, tm):
    ti = pl.program_id(1)
    prev = FFN_CONV - 1

    @pl.when(ti == 0)
    def _():
        carry_ref[SUBLANES - prev:SUBLANES, :] = conv0_ref[0]

    x = x_ref[...]
    xn = _rms(x, gpre_ref[...]).astype(BF16)

    def conv_cols(lo, w):
        up = _dot(xn, wup_ref[:, lo:lo + w])
        buf_ref[SUBLANES - prev:SUBLANES, 0:w] = carry_ref[SUBLANES - prev:SUBLANES, lo:lo + w]
        buf_ref[SUBLANES:SUBLANES + tm, 0:w] = up
        tail = up[tm - prev:tm, :]
        carry_ref[SUBLANES - prev:SUBLANES, lo:lo + w] = tail
        conv_out_ref[0, :, lo:lo + w] = tail
        return cb_ref[:, lo:lo + w] + sum(
            buf_ref[SUBLANES - prev + j:SUBLANES - prev + j + tm, 0:w] * cw_ref[j:j + 1, lo:lo + w]
            for j in range(FFN_CONV))

    for lo in range(0, D_FF, _FF_CHUNK):
        w = min(_FF_CHUNK, D_FF - lo)
        ua = conv_cols(lo, w)
        ub = conv_cols(D_FF + lo, w)
        hid_ref[:, lo:lo + w] = (_gelu_tanh(ua) * ub).astype(BF16)
    o_ref[...] = x + _rms(_dot(hid_ref[...], wdn_ref[...]), gpost_ref[...])


def _ffn(x1, conv0, W, b, t):
    tm = min(512, t)
    nt = t // tm
    row = pl.BlockSpec((tm, D_MODEL), lambda i, j: (i * nt + j, 0))
    st = pl.BlockSpec((1, FFN_CONV - 1, 2 * D_FF), lambda i, j: (i, 0, 0))
    return pl.pallas_call(
        functools.partial(_ffn_kernel, tm=tm),
        grid=(b, nt),
        in_specs=[row, _resident((1, D_MODEL)), _resident((D_MODEL, 2 * D_FF)),
                  _resident((FFN_CONV, 2 * D_FF)), _resident((1, 2 * D_FF)),
                  _resident((D_FF, D_MODEL)), _resident((1, D_MODEL)), st],
        out_specs=[row, st],
        out_shape=[jax.ShapeDtypeStruct((b * t, D_MODEL), F32),
                   jax.ShapeDtypeStruct((b, FFN_CONV - 1, 2 * D_FF), F32)],
        scratch_shapes=[pltpu.VMEM((SUBLANES + tm, _FF_CHUNK), F32),
                        pltpu.VMEM((SUBLANES, 2 * D_FF), F32), pltpu.VMEM((tm, D_FF), BF16)],
        compiler_params=_params("parallel", "arbitrary"),
        name="ffn",
    )(x1, W["norm_ffn_pre"], W["w_up"], W["ffn_conv_w"], W["ffn_conv_b"], W["w_down"],
      W["norm_ffn_post"], conv0)


def _prep_weights(norm_mix_pre, w_in, b_in, fox_q_norm, fox_k_norm, mlstm_conv_w, mlstm_conv_b,
                  mlstm_head_norm, w_br_a, w_br_b, w_br_m, w_out, norm_mix_post, norm_ffn_pre,
                  w_up, ffn_conv_w, ffn_conv_b, w_down, norm_ffn_post):
    o_ff, o_mq, o_mi, o_mf, o_mo, o_cq, o_g = 1536, 1544, 3080, 3084, 3088, 3600, 4112

    def pack(a):
        pad = jnp.zeros(a.shape[:-1] + (LANES - FOX_HEADS - 2 * ML_HEADS,), a.dtype)
        small = jnp.concatenate([a[..., o_ff:o_mq], a[..., o_mi:o_mf], a[..., o_mf:o_mo], pad], -1)
        return jnp.concatenate([a[..., :o_ff], a[..., o_mq:o_mi], a[..., o_mo:o_g], small], -1)

    row = lambda v: v.reshape(1, -1).astype(F32)
    head = jnp.arange(FOX_W) // FOX_DIM
    return {
        "norm_mix_pre": row(norm_mix_pre),
        "w_in_a": w_in[:, :o_ff].astype(BF16),
        "w_in_b": w_in[:, o_mq:o_mi].astype(BF16),
        "w_in_c": w_in[:, o_mo:o_g].astype(BF16),
        "w_in_s": jnp.concatenate([w_in[:, o_ff:o_mq], w_in[:, o_mi:o_mo],
                                   jnp.zeros((D_MODEL, LANES - FOX_HEADS - 2 * ML_HEADS), F32)], -1).astype(BF16),
        "b_in": row(pack(b_in)),
        "w_gate": w_in[:, o_g:].astype(BF16),
        "b_gate": row(b_in[o_g:]),
        "fox_q_norm": row(jnp.tile(fox_q_norm, FOX_HEADS)),
        "fox_k_norm": row(jnp.tile(fox_k_norm, FOX_HEADS)),
        "head_avg": ((head[:, None] == head[None, :]).astype(F32) / FOX_DIM).astype(BF16),
        "mlstm_conv_w": mlstm_conv_w.astype(F32),
        "mlstm_conv_b": row(mlstm_conv_b),
        "mlstm_head_norm": row(mlstm_head_norm),
        "w_br_a": w_br_a.astype(BF16), "w_br_b": w_br_b.astype(BF16), "w_br_m": w_br_m.astype(BF16),
        "w_out": w_out.astype(BF16),
        "norm_mix_post": row(norm_mix_post),
        "norm_ffn_pre": row(norm_ffn_pre),
        "w_up": w_up.astype(BF16),
        "ffn_conv_w": ffn_conv_w.astype(F32),
        "ffn_conv_b": row(ffn_conv_b),
        "w_down": w_down.astype(BF16),
        "norm_ffn_post": row(norm_ffn_post),
    }


def _rows_layout(a, b, t, r):
    return jnp.transpose(a[:, :r].reshape(b, t, r), (0, 2, 1))


def _layer(x, fox_cache, ml_c0, ml_n0, ml_m0, ml_conv_prev, mem_k, mem_v, ffn_conv_prev, W):
    b, t, d = x.shape
    n = b * t
    x2 = x.reshape(n, d)
    q, kb, vb, fk, fv, mqk, mv, mo, cq, sm = _in_proj(x2, W, b, t)
    if fk.shape == (b, FOX_W, t):
        fk, fv = (jnp.transpose(a.reshape(b, FOX_HEADS, FOX_DIM, t), (0, 3, 1, 2)) for a in (fk, fv))

    pairs = FOX_HEADS // 2
    f_rows = _rows_layout(sm, b, t, FOX_HEADS).reshape(b * FOX_HEADS, t)
    zeros_init = jnp.zeros((b * FOX_HEADS, LANES), F32)
    q3, fk3, fv3 = q.reshape(b, t, FOX_W), kb.reshape(b, t, FOX_W), vb.reshape(b, t, FOX_W)
    if fox_cache is None:
        c_new = _cumsum_rows(f_rows, zeros_init)
        a_out = _fox_attn_prompt(q3, fk3, fv3, c_new.reshape(b, pairs, 2, t))
    else:
        k_cache, v_cache, logf_cache = fox_cache
        plen = k_cache.shape[1]
        logf_rows = jnp.transpose(logf_cache.astype(F32), (0, 2, 1)).reshape(b * FOX_HEADS, plen)
        c_cache = _cumsum_rows(logf_rows, zeros_init)
        c_new = _cumsum_rows(f_rows, jnp.broadcast_to(c_cache[:, plen - 1:plen], (b * FOX_HEADS, LANES)))
        feature_major = lambda a: jnp.transpose(a.astype(F32), (0, 2, 3, 1)).reshape(b * FOX_W, plen)
        a_out = _fox_attn_cached(q3, fk3, fv3, c_new.reshape(b, FOX_HEADS, t), feature_major(k_cache),
                                 feature_major(v_cache), c_cache.reshape(b, FOX_HEADS, plen))

    smt = _rows_layout(sm, b, t, 16)
    b_out, ml_conv_new, c_t, n_t, m_t = _mlstm(
        mqk, mv, mo, sm, smt, W, ml_conv_prev.astype(F32), ml_c0.astype(F32),
        ml_n0.astype(F32).reshape(b, ML_HEADS, 1, ML_DIM),
        jnp.broadcast_to(ml_m0.astype(F32)[:, :, None, None], (b, ML_HEADS, 1, LANES)), b, t)

    mlen = mem_k.shape[1]
    m_out = _mem_attn(cq, mem_k.reshape(b, mlen, MEM_W), mem_v.reshape(b, mlen, MEM_W), b, t)

    x1 = _merge(x2, a_out.reshape(n, FOX_W), b_out, m_out, W)
    y, ffn_conv_new = _ffn(x1, ffn_conv_prev.astype(F32), W, b, t)

    states = (fk.reshape(b, t, FOX_HEADS, FOX_DIM), fv.reshape(b, t, FOX_HEADS, FOX_DIM),
              sm[:, :FOX_HEADS].reshape(b, t, FOX_HEADS), c_t, n_t.reshape(b, ML_HEADS, ML_DIM),
              m_t[:, :, 0, 0], ml_conv_new, ffn_conv_new)
    return y.reshape(b, t, d), states


def kernel(x_prompt, x_sample, cache_fox_k, cache_fox_v, cache_fox_logf, state_mlstm_c, state_mlstm_n, state_mlstm_m, state_mlstm_conv, cache_mem_k, cache_mem_v, state_ffn_conv, mem_prompt, norm_mix_pre, w_in, b_in, fox_q_norm, fox_k_norm, mlstm_conv_w, mlstm_conv_b, mlstm_head_norm, norm_mem, w_mem_kv, w_br_a, w_br_b, w_br_m, w_out, norm_mix_post, norm_ffn_pre, w_up, ffn_conv_w, ffn_conv_b, w_down, norm_ffn_post):
    depth = w_in.shape[0]
    hp, hs = x_prompt, x_sample
    b = x_prompt.shape[0]
    new_p = [[] for _ in range(10)]
    new_s = [[] for _ in range(8)]
    for l in range(depth):
        W = _prep_weights(norm_mix_pre[l], w_in[l], b_in[l], fox_q_norm[l], fox_k_norm[l],
                          mlstm_conv_w[l], mlstm_conv_b[l], mlstm_head_norm[l], w_br_a[l], w_br_b[l],
                          w_br_m[l], w_out[l], norm_mix_post[l], norm_ffn_pre[l], w_up[l],
                          ffn_conv_w[l], ffn_conv_b[l], w_down[l], norm_ffn_post[l])
        mlen = mem_prompt.shape[1]
        kv = _norm_matmul(mem_prompt.reshape(b * mlen, D_MODEL), norm_mem[l].reshape(1, -1).astype(F32),
                          w_mem_kv[l].astype(BF16))
        mem_k_p = kv[:, :MEM_W].reshape(b, mlen, MEM_HEADS, MEM_DIM)
        mem_v_p = kv[:, MEM_W:].reshape(b, mlen, MEM_HEADS, MEM_DIM)
        hp, st_p = _layer(
            hp, None,
            jnp.zeros((b, ML_HEADS, ML_DIM, ML_DIM), F32), jnp.zeros((b, ML_HEADS, ML_DIM), F32),
            jnp.zeros((b, ML_HEADS), F32), jnp.zeros((b, ML_CONV - 1, 2 * ML_W), F32),
            mem_k_p, mem_v_p, jnp.zeros((b, FFN_CONV - 1, 2 * D_FF), F32), W)
        hs, st_s = _layer(
            hs, (cache_fox_k[l], cache_fox_v[l], cache_fox_logf[l]),
            state_mlstm_c[l], state_mlstm_n[l], state_mlstm_m[l], state_mlstm_conv[l],
            cache_mem_k[l], cache_mem_v[l], state_ffn_conv[l], W)
        for acc, a in zip(new_p, st_p + (mem_k_p, mem_v_p)):
            acc.append(a)
        for acc, a in zip(new_s, st_s):
            acc.append(a)
    sp = [jnp.stack(a, axis=0) for a in new_p]
    ss = [jnp.stack(a, axis=0) for a in new_s]
    return (hp, hs, sp[0], sp[1], sp[2], sp[3], sp[4], sp[5], sp[6], sp[7], sp[8], sp[9],
            ss[0], ss[1], ss[2], ss[3], ss[4], ss[5], ss[6], ss[7])
```

```python
import functools

import jax
import jax.numpy as jnp
from jax import lax
from jax.experimental import pallas as pl
from jax.experimental.pallas import tpu as pltpu

F32 = jnp.float32
BF16 = jnp.bfloat16

D_MODEL = 1024
FOX_HEADS = 8
FOX_DIM = 64
ML_HEADS = 4
ML_DIM = 128
ML_CONV = 4
MEM_HEADS = 4
MEM_DIM = 128
D_FF = 2816
FFN_CONV = 3
N_BRANCH = 3
EPS = 1e-6
FOX_W = FOX_HEADS * FOX_DIM
ML_W = ML_HEADS * ML_DIM
MEM_W = MEM_HEADS * MEM_DIM

LANES = 128
SUBLANES = 8
NEG_BIG = -1e30
LOG2E = 1.4426950408889634
VMEM_LIMIT = 56 * 1024 * 1024

_C_FQ, _C_FK, _C_FV = 0, 512, 1024
_C_MQK, _C_MV, _C_MO, _C_CQ, _C_SM = 1536, 2560, 3072, 3584, 4096
_IN_COLS = 4224
_IN_GROUPS = ("w_in_a", "w_in_b", "w_in_c", "w_in_s")
_SM_FF, _SM_MI, _SM_MF = 0, 8, 12


def _params(*sem):
    return pltpu.CompilerParams(dimension_semantics=sem, vmem_limit_bytes=VMEM_LIMIT)


def _resident(shape):
    nd = len(shape)
    return pl.BlockSpec(shape, lambda *_: (0,) * nd, pipeline_mode=pl.Buffered(1))


def _rms(x, g):
    return x * lax.rsqrt(jnp.mean(x * x, axis=-1, keepdims=True) + EPS) * g


def _dot(a, b):
    return jnp.dot(a, b, preferred_element_type=F32)


def _dot_nt(a, b):
    return lax.dot_general(a, b, (((1,), (1,)), ((), ())), preferred_element_type=F32)


def _dot_tn(a, b):
    return lax.dot_general(a, b, (((0,), (0,)), ((), ())), preferred_element_type=F32)


def _log_sigmoid(z):
    return jnp.minimum(z, 0.0) - jnp.log1p(jnp.exp(-jnp.abs(z)))


def _sigmoid(z):
    return 1.0 / (1.0 + jnp.exp(-z))


def _gelu_tanh(x):
    c = 0.7978845608028654
    hx = 0.5 * x
    return hx + hx * jnp.tanh(x * (c + (c * 0.044715) * (x * x)))


def _in_proj_kernel(x_ref, g_ref, wa_ref, wb_ref, wc_ref, ws_ref, b_ref, qn_ref, kn_ref, bd_ref,
                    q_ref, kb_ref, vb_ref, fk_ref, fv_ref, mqk_ref, mv_ref, mo_ref, cq_ref, sm_ref,
                    *, feature_major):
    xn = _rms(x_ref[...], g_ref[...]).astype(BF16)

    groups = ((_C_FQ, wa_ref), (_C_MQK, wb_ref), (_C_MO, wc_ref), (_C_SM, ws_ref))

    def proj(lo, hi):
        base, w_ref = [g for g in groups if g[0] <= lo][-1]
        return _dot(xn, w_ref[:, lo - base:hi - base]) + b_ref[:, lo:hi]

    def head_norm(z, gn):
        ms = _dot((z * z).astype(BF16), bd_ref[...])
        return z * lax.rsqrt(ms + EPS) * gn

    def store_heads(dense_ref, state_ref, val):
        dense_ref[...] = val.astype(BF16)
        if feature_major:
            state_ref[0] = val.T
        else:
            for h in range(FOX_HEADS):
                state_ref[:, h, :] = val[:, h * FOX_DIM:(h + 1) * FOX_DIM]

    q_ref[...] = (head_norm(proj(_C_FQ, _C_FK), qn_ref[...]) * (FOX_DIM ** -0.5 * LOG2E)).astype(BF16)
    store_heads(kb_ref, fk_ref, head_norm(proj(_C_FK, _C_FV), kn_ref[...]))
    store_heads(vb_ref, fv_ref, proj(_C_FV, _C_MQK))
    mqk_ref[...] = proj(_C_MQK, _C_MV)
    mv_ref[...] = proj(_C_MV, _C_MO).astype(BF16)
    mo_ref[...] = proj(_C_MO, _C_CQ)
    cq_ref[...] = (proj(_C_CQ, _C_SM) * (MEM_DIM ** -0.5)).astype(BF16)
    z = proj(_C_SM, _IN_COLS)
    lane = lax.broadcasted_iota(jnp.int32, z.shape, 1)
    is_forget = (lane < _SM_MI) | ((lane >= _SM_MF) & (lane < _SM_MF + ML_HEADS))
    sm_ref[...] = jnp.where(is_forget, _log_sigmoid(z), z)


def _in_proj(x2, W, b, t):
    n = x2.shape[0]
    tm = min(512, n)
    row = lambda w: pl.BlockSpec((tm, w), lambda i: (i, 0))
    feature_major = t % tm == 0
    if feature_major:
        nt = t // tm
        heads = pl.BlockSpec((1, FOX_W, tm), lambda i: (i // nt, 0, i % nt))
        heads_shape = jax.ShapeDtypeStruct((b, FOX_W, t), F32)
    else:
        heads = pl.BlockSpec((tm, FOX_HEADS, FOX_DIM), lambda i: (i, 0, 0))
        heads_shape = jax.ShapeDtypeStruct((n, FOX_HEADS, FOX_DIM), F32)
    outs = [(FOX_W, BF16), (FOX_W, BF16), (FOX_W, BF16), None, None, (2 * ML_W, F32), (ML_W, BF16),
            (ML_W, F32), (MEM_W, BF16), (LANES, F32)]
    return pl.pallas_call(
        functools.partial(_in_proj_kernel, feature_major=feature_major),
        grid=(n // tm,),
        in_specs=[row(D_MODEL), _resident((1, D_MODEL))] + [_resident(W[k].shape) for k in _IN_GROUPS]
                 + [_resident((1, _IN_COLS)), _resident((1, FOX_W)), _resident((1, FOX_W)),
                    _resident((FOX_W, FOX_W))],
        out_specs=[heads if o is None else row(o[0]) for o in outs],
        out_shape=[heads_shape if o is None else jax.ShapeDtypeStruct((n, o[0]), o[1]) for o in outs],
        compiler_params=_params("parallel"),
        name="in_proj",
    )(x2, W["norm_mix_pre"], *[W[k] for k in _IN_GROUPS], W["b_in"], W["fox_q_norm"], W["fox_k_norm"],
      W["head_avg"])


def _norm_matmul_kernel(x_ref, g_ref, w_ref, o_ref):
    o_ref[...] = _dot(_rms(x_ref[...], g_ref[...]).astype(BF16), w_ref[...])


def _norm_matmul(x2, g, w):
    n, d = x2.shape
    cols = w.shape[1]
    tm = min(512, n)
    return pl.pallas_call(
        _norm_matmul_kernel,
        grid=(n // tm,),
        in_specs=[pl.BlockSpec((tm, d), lambda i: (i, 0)), _resident((1, d)), _resident((d, cols))],
        out_specs=pl.BlockSpec((tm, cols), lambda i: (i, 0)),
        out_shape=jax.ShapeDtypeStruct((n, cols), F32),
        compiler_params=_params("parallel"),
        name="mem_kv",
    )(x2, g, w)


def _cumsum_rows_kernel(f_ref, init_ref, o_ref, carry_ref):
    @pl.when(pl.program_id(0) == 0)
    def _():
        carry_ref[...] = init_ref[...]

    tb = f_ref.shape[-1]
    r = lax.broadcasted_iota(jnp.int32, (tb, tb), 0)
    c = lax.broadcasted_iota(jnp.int32, (tb, tb), 1)
    upper = (r <= c).astype(F32)
    cs = jnp.dot(f_ref[...] * LOG2E, upper, precision=lax.Precision.HIGHEST,
                 preferred_element_type=F32) + carry_ref[:, 0:1]
    o_ref[...] = cs
    carry_ref[...] = jnp.broadcast_to(cs[:, tb - 1:tb], carry_ref.shape)


def _cumsum_rows(f_rows, init):
    r, t = f_rows.shape
    tb = min(256, t)
    return pl.pallas_call(
        _cumsum_rows_kernel,
        grid=(t // tb,),
        in_specs=[pl.BlockSpec((r, tb), lambda j: (0, j)), _resident((r, LANES))],
        out_specs=pl.BlockSpec((r, tb), lambda j: (0, j)),
        out_shape=jax.ShapeDtypeStruct((r, t), F32),
        scratch_shapes=[pltpu.VMEM((r, LANES), F32)],
        compiler_params=_params("arbitrary"),
        name="cumsum_rows",
    )(f_rows, init)


def _stack_heads(q):
    lane = lax.broadcasted_iota(jnp.int32, q.shape, 1)
    zero = jnp.zeros_like(q)
    return jnp.concatenate([jnp.where(lane < FOX_DIM, q, zero), jnp.where(lane >= FOX_DIM, q, zero)], axis=0)


def _unstack_heads(o):
    tq = o.shape[0] // 2
    lane = lax.broadcasted_iota(jnp.int32, (tq, LANES), 1)
    return jnp.where(lane < FOX_DIM, o[:tq], o[tq:])


def _softmax_update(s, bias, mask, m_ref, l_ref):
    tq = s.shape[0] // len(bias)
    tk = s.shape[1]
    parts = [s[i * tq:(i + 1) * tq] + b for i, b in enumerate(bias)]
    s = jnp.concatenate(parts, axis=0) if len(parts) > 1 else parts[0]
    if mask is not None:
        s = jnp.where(mask, s, NEG_BIG)
    tiles = [s[:, i * LANES:(i + 1) * LANES] for i in range(tk // LANES)] if tk >= LANES else [s]
    m_cur = tiles[0]
    for tl in tiles[1:]:
        m_cur = jnp.maximum(m_cur, tl)
    m_prev = m_ref[...]
    m_new = jnp.maximum(m_prev, jnp.max(m_cur, axis=-1, keepdims=True))
    alpha = jnp.exp2(m_prev - m_new)
    if tk >= LANES:
        ps = [jnp.exp2(tl - m_new) for tl in tiles]
        l_ref[...] = alpha * l_ref[...] + sum(ps[1:], ps[0])
        p = jnp.concatenate(ps, axis=1) if len(ps) > 1 else ps[0]
    else:
        p = jnp.exp2(s - m_new[:, :tk])
        l_scaled = alpha * l_ref[...]
        l_ref[...] = l_scaled
        l_ref[:, :tk] = l_scaled[:, :tk] + p
    m_ref[...] = m_new
    return p.astype(BF16), alpha


def _init_stats(m_ref, l_ref, acc_ref):
    m_ref[...] = jnp.full(m_ref.shape, NEG_BIG, F32)
    l_ref[...] = jnp.zeros(l_ref.shape, F32)
    acc_ref[...] = jnp.zeros(acc_ref.shape, F32)


def _causal_stacked(tq):
    rr = lax.broadcasted_iota(jnp.int32, (2 * tq, tq), 0)
    cc = lax.broadcasted_iota(jnp.int32, (2 * tq, tq), 1)
    return cc <= jnp.where(rr >= tq, rr - tq, rr)


def _fox_prompt_kernel(q_ref, kb_ref, vb_ref, cn_ref, o_ref, q2_ref, m_ref, l_ref, acc_ref, s_ref, p_ref,
                       *, tq, nq):
    stages = [(qi, j) for qi in range(nq) for j in range(qi + 1)]
    rows = lambda i: slice(i * tq, (i + 1) * tq)
    causal = _causal_stacked(tq)

    def scores(k):
        qi, j = stages[k]
        if j == 0:
            q2_ref[qi % 2] = _stack_heads(q_ref[0, rows(qi), :])
        s_ref[k % 2] = _dot_nt(q2_ref[qi % 2], kb_ref[0, rows(j), :])

    def finish(qi, acc):
        o = acc / jnp.sum(l_ref[qi % 2], axis=-1, keepdims=True)
        o_ref[0, rows(qi), :] = _unstack_heads(o).astype(BF16)

    scores(0)
    for k, (qi, j) in enumerate(stages):
        st = qi % 2
        if k + 1 < len(stages):
            scores(k + 1)
        if k > 0:
            qi_prev, j_prev = stages[k - 1]
            pv_prev = _dot(p_ref[(k - 1) % 2], vb_ref[0, rows(j_prev), :])
            if qi_prev != qi:
                finish(qi_prev, (acc_ref[qi_prev % 2] + pv_prev) if j_prev > 0 else pv_prev)
        if j == 0:
            m_ref[st] = jnp.full(m_ref.shape[1:], NEG_BIG, F32)
            l_ref[st] = jnp.zeros(l_ref.shape[1:], F32)
        bias = [cn_ref[0, 0, hh:hh + 1, qi * tq:qi * tq + 1] - cn_ref[0, 0, hh:hh + 1, rows(j)] for hh in range(2)]
        p_cur, alpha = _softmax_update(s_ref[k % 2], bias, causal if j == qi else None, m_ref.at[st], l_ref.at[st])
        p_ref[k % 2] = p_cur
        if j > 0:
            acc_ref[st] = alpha * ((acc_ref[st] + pv_prev) if j > 1 else pv_prev)
    qi, j = stages[-1]
    pv_last = _dot(p_ref[(len(stages) - 1) % 2], vb_ref[0, rows(j), :])
    finish(qi, (acc_ref[qi % 2] + pv_last) if j > 0 else pv_last)


def _fox_attn_prompt(q, k_new, v_new, c_new):
    b, t, _ = q.shape
    tq = min(512, t)
    seq = pl.BlockSpec((1, t, LANES), lambda i, p: (i, 0, p))
    stat = pltpu.VMEM((2, 2 * tq, LANES), F32)
    return pl.pallas_call(
        functools.partial(_fox_prompt_kernel, tq=tq, nq=t // tq),
        grid=(b, FOX_HEADS // 2),
        in_specs=[seq, seq, seq, pl.BlockSpec((1, 1, 2, t), lambda i, p: (i, p, 0, 0))],
        out_specs=seq,
        out_shape=jax.ShapeDtypeStruct((b, t, FOX_W), BF16),
        scratch_shapes=[pltpu.VMEM((2, 2 * tq, LANES), BF16), stat, stat, stat,
                        pltpu.VMEM((2, 2 * tq, tq), F32), pltpu.VMEM((2, 2 * tq, tq), BF16)],
        compiler_params=_params("parallel", "parallel"),
        name="fox_attn_prompt",
    )(q, k_new, v_new, c_new)


def _fox_cached_kernel(q_ref, kn_ref, vn_ref, cn_ref, kc_ref, vc_ref, cc_ref, o_ref,
                       qh_ref, m_ref, l_ref, acc_ref, *, t, nblk, tkc):
    j = pl.program_id(1)
    head = lambda h: slice(h * FOX_DIM, (h + 1) * FOX_DIM)

    @pl.when(j == 0)
    def _():
        for h in range(FOX_HEADS):
            qh_ref[h] = q_ref[0, :, head(h)]
        _init_stats(m_ref, l_ref, acc_ref)

    def attend_all(scores, c_rows, mask, weighted_values):
        s_all = [scores(h) for h in range(FOX_HEADS)]
        stats = [_softmax_update(s_all[h], [cn_ref[0, h:h + 1, 0:1] - c_rows[0, h:h + 1, :]], mask,
                                 m_ref.at[h], l_ref.at[h]) for h in range(FOX_HEADS)]
        for h, (p, alpha) in enumerate(stats):
            acc_ref[h] = alpha[:, :FOX_DIM] * acc_ref[h] + weighted_values(h, p)

    attend_all(lambda h: _dot(qh_ref[h], kc_ref[head(h), :].astype(BF16)), cc_ref, None,
               lambda h, p: _dot_nt(p, vc_ref[head(h), :].astype(BF16)))

    @pl.when(j == nblk - 1)
    def _():
        rr = lax.broadcasted_iota(jnp.int32, (t, t), 0)
        cc = lax.broadcasted_iota(jnp.int32, (t, t), 1)
        attend_all(lambda h: _dot_nt(qh_ref[h], kn_ref[0, :, head(h)]), cn_ref, cc <= rr,
                   lambda h, p: _dot(p, vn_ref[0, :, head(h)]))
        for h in range(FOX_HEADS):
            o = acc_ref[h] / jnp.sum(l_ref[h], axis=-1, keepdims=True)
            o_ref[0, :, head(h)] = o.astype(BF16)


def _fox_attn_cached(q, k_new, v_new, c_new, k_cache, v_cache, c_cache):
    b, t, _ = q.shape
    plen = c_cache.shape[-1]
    tkc = min(1024, plen)
    nblk = plen // tkc
    new = pl.BlockSpec((1, t, FOX_W), lambda i, j: (i, 0, 0))
    cache = pl.BlockSpec((FOX_W, tkc), lambda i, j: (i, j))
    return pl.pallas_call(
        functools.partial(_fox_cached_kernel, t=t, nblk=nblk, tkc=tkc),
        grid=(b, nblk),
        in_specs=[new, new, new, pl.BlockSpec((1, FOX_HEADS, t), lambda i, j: (i, 0, 0)),
                  cache, cache, pl.BlockSpec((1, FOX_HEADS, tkc), lambda i, j: (i, 0, j))],
        out_specs=new,
        out_shape=jax.ShapeDtypeStruct((b, t, FOX_W), BF16),
        scratch_shapes=[pltpu.VMEM((FOX_HEADS, t, FOX_DIM), BF16), pltpu.VMEM((FOX_HEADS, t, LANES), F32),
                        pltpu.VMEM((FOX_HEADS, t, LANES), F32), pltpu.VMEM((FOX_HEADS, t, FOX_DIM), F32)],
        compiler_params=_params("parallel", "arbitrary"),
        name="fox_attn_cached",
    )(q, k_new, v_new, c_new, k_cache, v_cache, c_cache)


_XP_HEAD = SUBLANES


def _mlstm_kernel(mqk_ref, mv_ref, mo_ref, sm_ref, smt_ref, cw_ref, cb_ref, hn_ref,
                  conv0_ref, c0_ref, n0_ref, m0_ref,
                  o_ref, conv_out_ref, c_out_ref, n_out_ref, m_out_ref,
                  xp_ref, c_ref, n_ref, m_ref, *, L):
    ci = pl.program_id(1)
    prev = ML_CONV - 1

    @pl.when(ci == 0)
    def _():
        xp_ref[_XP_HEAD - prev:_XP_HEAD, :] = conv0_ref[0]
        c_ref[...] = c0_ref[0]
        n_ref[...] = n0_ref[0]
        m_ref[...] = m0_ref[0]

    xp_ref[_XP_HEAD:_XP_HEAD + L, :] = mqk_ref[...]
    y = cb_ref[...] + sum(xp_ref[_XP_HEAD - prev + j:_XP_HEAD - prev + j + L, :] * cw_ref[j:j + 1, :]
                          for j in range(ML_CONV))
    qk = y * _sigmoid(y)
    tail = xp_ref[_XP_HEAD + L - prev:_XP_HEAD + L, :]
    conv_out_ref[0] = tail
    xp_ref[_XP_HEAD - prev:_XP_HEAD, :] = tail

    rr = lax.broadcasted_iota(jnp.int32, (L, L), 0)
    cc = lax.broadcasted_iota(jnp.int32, (L, L), 1)
    causal = cc <= rr
    sm = sm_ref[...]
    smt = smt_ref[0]
    hi = lax.Precision.HIGHEST
    bt_cols = jnp.dot(causal.astype(F32), sm, precision=hi, preferred_element_type=F32)
    bt_rows = jnp.dot(smt, (rr <= cc).astype(F32), precision=hi, preferred_element_type=F32)

    heads = range(ML_HEADS)
    sl = [slice(h * ML_DIM, (h + 1) * ML_DIM) for h in heads]
    q = [qk[:, sl[h]] * (ML_DIM ** -0.5) for h in heads]
    k = [qk[:, ML_W + h * ML_DIM:ML_W + (h + 1) * ML_DIM] for h in heads]
    qb = [a.astype(BF16) for a in q]
    kb = [a.astype(BF16) for a in k]
    c_p = [c_ref[h] for h in heads]
    n_p = [n_ref[h] for h in heads]
    m_p = [m_ref[h][:, 0:1] for h in heads]
    qk_scores = [_dot_nt(qb[h], kb[h]) for h in heads]
    q_c = [_dot_nt(qb[h], c_p[h].astype(BF16)) for h in heads]

    s_list, w_inter, den, m_t, wg, g_max, b_end = [], [], [], [], [], [], []
    for h in heads:
        it_col = sm[:, _SM_MI + h:_SM_MI + h + 1]
        it_row = smt[_SM_MI + h:_SM_MI + h + 1, :]
        bt_col = bt_cols[:, _SM_MF + h:_SM_MF + h + 1]
        bt_row = bt_rows[_SM_MF + h:_SM_MF + h + 1, :]
        b_end.append(bt_col[L - 1:L, :])
        log_w = bt_col - bt_row + it_row
        m_intra = jnp.max(jnp.where(causal, log_w, NEG_BIG), axis=-1, keepdims=True)
        log_inter = bt_col + m_p[h]
        m_t.append(jnp.maximum(log_inter, m_intra))
        dmat = jnp.where(causal, jnp.exp(log_w - m_t[h]), 0.0)
        s_list.append(qk_scores[h] * dmat)
        w_inter.append(jnp.exp(log_inter - m_t[h]))
        den.append(w_inter[h] * jnp.sum(q[h] * n_p[h], axis=-1, keepdims=True)
                   + jnp.sum(s_list[h], axis=-1, keepdims=True))
        g_col = b_end[h] - bt_col + it_col
        g_max.append(jnp.max(g_col, axis=0, keepdims=True))
        wg.append(jnp.exp(g_col - g_max[h]))

    s_v = [_dot(s_list[h].astype(BF16), mv_ref[:, sl[h]]) for h in heads]
    kv_blk = [_dot_tn((mv_ref[:, sl[h]].astype(F32) * wg[h]).astype(BF16), kb[h]) for h in heads]

    for h in heads:
        num = w_inter[h] * q_c[h] + s_v[h]
        hout = num / jnp.maximum(jnp.abs(den[h]), jnp.exp(-m_t[h]))
        hg = hout * _sigmoid(mo_ref[:, sl[h]])
        o_ref[:, sl[h]] = _rms(hg, hn_ref[:, sl[h]]).astype(BF16)

        k_blk = jnp.sum(k[h] * wg[h], axis=0, keepdims=True)
        m_new = jnp.maximum(b_end[h] + m_p[h], g_max[h])
        decay = jnp.exp(b_end[h] + m_p[h] - m_new)
        scale = jnp.exp(g_max[h] - m_new)
        c_new = decay * c_p[h] + scale * kv_blk[h]
        n_new = decay * n_p[h] + scale * k_blk
        c_ref[h] = c_new
        n_ref[h] = n_new
        m_ref[h] = jnp.broadcast_to(m_new, (1, LANES))
        c_out_ref[0, h] = c_new
        n_out_ref[0, h] = n_new
        m_out_ref[0, h] = jnp.broadcast_to(m_new, (1, LANES))


def _mlstm(mqk, mv, mo, sm, smt, W, conv0, c0, n0, m0, b, t):
    L = min(256, t)
    nc = t // L
    row = lambda w: pl.BlockSpec((L, w), lambda i, j: (i * nc + j, 0))
    st = lambda *shape: pl.BlockSpec((1,) + shape, lambda i, j: (i,) + (0,) * len(shape))
    return pl.pallas_call(
        functools.partial(_mlstm_kernel, L=L),
        grid=(b, nc),
        in_specs=[row(2 * ML_W), row(ML_W), row(ML_W), row(LANES),
                  pl.BlockSpec((1, 16, L), lambda i, j: (i, 0, j)),
                  _resident((ML_CONV, 2 * ML_W)), _resident((1, 2 * ML_W)), _resident((1, ML_W)),
                  st(ML_CONV - 1, 2 * ML_W), st(ML_HEADS, ML_DIM, ML_DIM),
                  st(ML_HEADS, 1, ML_DIM), st(ML_HEADS, 1, LANES)],
        out_specs=[row(ML_W), st(ML_CONV - 1, 2 * ML_W), st(ML_HEADS, ML_DIM, ML_DIM),
                   st(ML_HEADS, 1, ML_DIM), st(ML_HEADS, 1, LANES)],
        out_shape=[jax.ShapeDtypeStruct((b * t, ML_W), BF16),
                   jax.ShapeDtypeStruct((b, ML_CONV - 1, 2 * ML_W), F32),
                   jax.ShapeDtypeStruct((b, ML_HEADS, ML_DIM, ML_DIM), F32),
                   jax.ShapeDtypeStruct((b, ML_HEADS, 1, ML_DIM), F32),
                   jax.ShapeDtypeStruct((b, ML_HEADS, 1, LANES), F32)],
        scratch_shapes=[pltpu.VMEM((_XP_HEAD + L, 2 * ML_W), F32),
                        pltpu.VMEM((ML_HEADS, ML_DIM, ML_DIM), F32),
                        pltpu.VMEM((ML_HEADS, 1, ML_DIM), F32),
                        pltpu.VMEM((ML_HEADS, 1, LANES), F32)],
        compiler_params=_params("parallel", "arbitrary"),
        name="mlstm",
    )(mqk, mv, mo, sm, smt, W["mlstm_conv_w"], W["mlstm_conv_b"], W["mlstm_head_norm"],
      conv0, c0, n0, m0)


def _mem_attn_kernel(q_ref, k_ref, v_ref, o_ref):
    sl = [slice(h * MEM_DIM, (h + 1) * MEM_DIM) for h in range(MEM_HEADS)]
    s = [_dot_nt(q_ref[:, c], k_ref[0, :, c].astype(BF16)) for c in sl]
    p = [jnp.exp(a - jnp.max(a, axis=-1, keepdims=True)) for a in s]
    pv = [_dot(a.astype(BF16), v_ref[0, :, c].astype(BF16)) for a, c in zip(p, sl)]
    for a, o, c in zip(p, pv, sl):
        o_ref[:, c] = (o / jnp.sum(a, axis=-1, keepdims=True)).astype(BF16)


def _mem_attn(cq, mem_k, mem_v, b, t):
    tm = min(512, t)
    nt = t // tm
    mlen = mem_k.shape[1]
    kv = pl.BlockSpec((1, mlen, MEM_W), lambda i: (i // nt, 0, 0))
    return pl.pallas_call(
        _mem_attn_kernel,
        grid=(b * nt,),
        in_specs=[pl.BlockSpec((tm, MEM_W), lambda i: (i, 0)), kv, kv],
        out_specs=pl.BlockSpec((tm, MEM_W), lambda i: (i, 0)),
        out_shape=jax.ShapeDtypeStruct((b * t, MEM_W), BF16),
        compiler_params=_params("parallel"),
        name="mem_attn",
    )(cq, mem_k, mem_v)


def _merge_kernel(x_ref, a_ref, b_ref, m_ref, gpre_ref, wg_ref, bg_ref, wa_ref, wb_ref, wm_ref,
                  wo_ref, gpost_ref, o_ref):
    x = x_ref[...]
    xn = _rms(x, gpre_ref[...]).astype(BF16)
    merged = None
    for i, (br_ref, w_ref) in enumerate(((a_ref, wa_ref), (b_ref, wb_ref), (m_ref, wm_ref))):
        sl = slice(i * D_MODEL, (i + 1) * D_MODEL)
        gate = _sigmoid(_dot(xn, wg_ref[:, sl]) + bg_ref[:, sl])
        term = gate * _dot(br_ref[...], w_ref[...])
        merged = term if merged is None else merged + term
    o_ref[...] = x + _rms(_dot(merged.astype(BF16), wo_ref[...]), gpost_ref[...])


def _merge(x2, a, b, m, W):
    n = x2.shape[0]
    tm = min(512, n)
    row = lambda w: pl.BlockSpec((tm, w), lambda i: (i, 0))
    return pl.pallas_call(
        _merge_kernel,
        grid=(n // tm,),
        in_specs=[row(D_MODEL), row(FOX_W), row(ML_W), row(MEM_W), _resident((1, D_MODEL)),
                  _resident((D_MODEL, N_BRANCH * D_MODEL)), _resident((1, N_BRANCH * D_MODEL)),
                  _resident((FOX_W, D_MODEL)), _resident((ML_W, D_MODEL)), _resident((MEM_W, D_MODEL)),
                  _resident((D_MODEL, D_MODEL)), _resident((1, D_MODEL))],
        out_specs=row(D_MODEL),
        out_shape=jax.ShapeDtypeStruct((n, D_MODEL), F32),
        compiler_params=_params("parallel"),
        name="merge",
    )(x2, a, b, m, W["norm_mix_pre"], W["w_gate"], W["b_gate"], W["w_br_a"], W["w_br_b"],
      W["w_br_m"], W["w_out"], W["norm_mix_post"])


_FF_CHUNK = 512


def _ffn_kernel(x_ref, gpre_ref, wup_ref, cw_ref, cb_ref, wdn_ref, gpost_ref, conv0_ref,
                o_ref, conv_out_ref, buf_ref, carry_ref, hid_ref, *, tm):
    ti = pl.program_id(1)
    prev = FFN_CONV - 1

    @pl.when(ti == 0)
    def _():
        carry_ref[SUBLANES - prev:SUBLANES, :] = conv0_ref[0]

    x = x_ref[...]
    xn = _rms(x, gpre_ref[...]).astype(BF16)

    def conv_cols(lo, w):
        up = _dot(xn, wup_ref[:, lo:lo + w])
        buf_ref[SUBLANES - prev:SUBLANES, 0:w] = carry_ref[SUBLANES - prev:SUBLANES, lo:lo + w]
        buf_ref[SUBLANES:SUBLANES + tm, 0:w] = up
        tail = up[tm - prev:tm, :]
        carry_ref[SUBLANES - prev:SUBLANES, lo:lo + w] = tail
        conv_out_ref[0, :, lo:lo + w] = tail
        return cb_ref[:, lo:lo + w] + sum(
            buf_ref[SUBLANES - prev + j:SUBLANES - prev + j + tm, 0:w] * cw_ref[j:j + 1, lo:lo + w]
            for j in range(FFN_CONV))

    for lo in range(0, D_FF, _FF_CHUNK):
        w = min(_FF_CHUNK, D_FF - lo)
        ua = conv_cols(lo, w)
        ub = conv_cols(D_FF + lo, w)
        hid_ref[:, lo:lo + w] = (_gelu_tanh(ua) * ub).astype(BF16)
    o_ref[...] = x + _rms(_dot(hid_ref[...], wdn_ref[...]), gpost_ref[...])


def _ffn(x1, conv0, W, b, t):
    tm = min(512, t)
    nt = t // tm
    row = pl.BlockSpec((tm, D_MODEL), lambda i, j: (i * nt + j, 0))
    st = pl.BlockSpec((1, FFN_CONV - 1, 2 * D_FF), lambda i, j: (i, 0, 0))
    return pl.pallas_call(
        functools.partial(_ffn_kernel, tm=tm),
        grid=(b, nt),
        in_specs=[row, _resident((1, D_MODEL)), _resident((D_MODEL, 2 * D_FF)),
                  _resident((FFN_CONV, 2 * D_FF)), _resident((1, 2 * D_FF)),
                  _resident((D_FF, D_MODEL)), _resident((1, D_MODEL)), st],
        out_specs=[row, st],
        out_shape=[jax.ShapeDtypeStruct((b * t, D_MODEL), F32),
                   jax.ShapeDtypeStruct((b, FFN_CONV - 1, 2 * D_FF), F32)],
        scratch_shapes=[pltpu.VMEM((SUBLANES + tm, _FF_CHUNK), F32),
                        pltpu.VMEM((SUBLANES, 2 * D_FF), F32), pltpu.VMEM((tm, D_FF), BF16)],
        compiler_params=_params("parallel", "arbitrary"),
        name="ffn",
    )(x1, W["norm_ffn_pre"], W["w_up"], W["ffn_conv_w"], W["ffn_conv_b"], W["w_down"],
      W["norm_ffn_post"], conv0)


def _prep_weights(norm_mix_pre, w_in, b_in, fox_q_norm, fox_k_norm, mlstm_conv_w, mlstm_conv_b,
                  mlstm_head_norm, w_br_a, w_br_b, w_br_m, w_out, norm_mix_post, norm_ffn_pre,
                  w_up, ffn_conv_w, ffn_conv_b, w_down, norm_ffn_post):
    o_ff, o_mq, o_mi, o_mf, o_mo, o_cq, o_g = 1536, 1544, 3080, 3084, 3088, 3600, 4112

    def pack(a):
        pad = jnp.zeros(a.shape[:-1] + (LANES - FOX_HEADS - 2 * ML_HEADS,), a.dtype)
        small = jnp.concatenate([a[..., o_ff:o_mq], a[..., o_mi:o_mf], a[..., o_mf:o_mo], pad], -1)
        return jnp.concatenate([a[..., :o_ff], a[..., o_mq:o_mi], a[..., o_mo:o_g], small], -1)

    row = lambda v: v.reshape(1, -1).astype(F32)
    head = jnp.arange(FOX_W) // FOX_DIM
    return {
        "norm_mix_pre": row(norm_mix_pre),
        "w_in_a": w_in[:, :o_ff].astype(BF16),
        "w_in_b": w_in[:, o_mq:o_mi].astype(BF16),
        "w_in_c": w_in[:, o_mo:o_g].astype(BF16),
        "w_in_s": jnp.concatenate([w_in[:, o_ff:o_mq], w_in[:, o_mi:o_mo],
                                   jnp.zeros((D_MODEL, LANES - FOX_HEADS - 2 * ML_HEADS), F32)], -1).astype(BF16),
        "b_in": row(pack(b_in)),
        "w_gate": w_in[:, o_g:].astype(BF16),
        "b_gate": row(b_in[o_g:]),
        "fox_q_norm": row(jnp.tile(fox_q_norm, FOX_HEADS)),
        "fox_k_norm": row(jnp.tile(fox_k_norm, FOX_HEADS)),
        "head_avg": ((head[:, None] == head[None, :]).astype(F32) / FOX_DIM).astype(BF16),
        "mlstm_conv_w": mlstm_conv_w.astype(F32),
        "mlstm_conv_b": row(mlstm_conv_b),
        "mlstm_head_norm": row(mlstm_head_norm),
        "w_br_a": w_br_a.astype(BF16), "w_br_b": w_br_b.astype(BF16), "w_br_m": w_br_m.astype(BF16),
        "w_out": w_out.astype(BF16),
        "norm_mix_post": row(norm_mix_post),
        "norm_ffn_pre": row(norm_ffn_pre),
        "w_up": w_up.astype(BF16),
        "ffn_conv_w": ffn_conv_w.astype(F32),
        "ffn_conv_b": row(ffn_conv_b),
        "w_down": w_down.astype(BF16),
        "norm_ffn_post": row(norm_ffn_post),
    }


def _rows_layout(a, b, t, r):
    return jnp.transpose(a[:, :r].reshape(b, t, r), (0, 2, 1))


def _layer(x, fox_cache, ml_c0, ml_n0, ml_m0, ml_conv_prev, mem_k, mem_v, ffn_conv_prev, W):
    b, t, d = x.shape
    n = b * t
    x2 = x.reshape(n, d)
    q, kb, vb, fk, fv, mqk, mv, mo, cq, sm = _in_proj(x2, W, b, t)
    if fk.shape == (b, FOX_W, t):
        fk, fv = (jnp.transpose(a.reshape(b, FOX_HEADS, FOX_DIM, t), (0, 3, 1, 2)) for a in (fk, fv))

    pairs = FOX_HEADS // 2
    f_rows = _rows_layout(sm, b, t, FOX_HEADS).reshape(b * FOX_HEADS, t)
    zeros_init = jnp.zeros((b * FOX_HEADS, LANES), F32)
    q3, fk3, fv3 = q.reshape(b, t, FOX_W), kb.reshape(b, t, FOX_W), vb.reshape(b, t, FOX_W)
    if fox_cache is None:
        c_new = _cumsum_rows(f_rows, zeros_init)
        a_out = _fox_attn_prompt(q3, fk3, fv3, c_new.reshape(b, pairs, 2, t))
    else:
        k_cache, v_cache, logf_cache = fox_cache
        plen = k_cache.shape[1]
        logf_rows = jnp.transpose(logf_cache.astype(F32), (0, 2, 1)).reshape(b * FOX_HEADS, plen)
        c_cache = _cumsum_rows(logf_rows, zeros_init)
        c_new = _cumsum_rows(f_rows, jnp.broadcast_to(c_cache[:, plen - 1:plen], (b * FOX_HEADS, LANES)))
        feature_major = lambda a: jnp.transpose(a.astype(F32), (0, 2, 3, 1)).reshape(b * FOX_W, plen)
        a_out = _fox_attn_cached(q3, fk3, fv3, c_new.reshape(b, FOX_HEADS, t), feature_major(k_cache),
                                 feature_major(v_cache), c_cache.reshape(b, FOX_HEADS, plen))

    smt = _rows_layout(sm, b, t, 16)
    b_out, ml_conv_new, c_t, n_t, m_t = _mlstm(
        mqk, mv, mo, sm, smt, W, ml_conv_prev.astype(F32), ml_c0.astype(F32),
        ml_n0.astype(F32).reshape(b, ML_HEADS, 1, ML_DIM),
        jnp.broadcast_to(ml_m0.astype(F32)[:, :, None, None], (b, ML_HEADS, 1, LANES)), b, t)

    mlen = mem_k.shape[1]
    m_out = _mem_attn(cq, mem_k.reshape(b, mlen, MEM_W), mem_v.reshape(b, mlen, MEM_W), b, t)

    x1 = _merge(x2, a_out.reshape(n, FOX_W), b_out, m_out, W)
    y, ffn_conv_new = _ffn(x1, ffn_conv_prev.astype(F32), W, b, t)

    states = (fk.reshape(b, t, FOX_HEADS, FOX_DIM), fv.reshape(b, t, FOX_HEADS, FOX_DIM),
              sm[:, :FOX_HEADS].reshape(b, t, FOX_HEADS), c_t, n_t.reshape(b, ML_HEADS, ML_DIM),
              m_t[:, :, 0, 0], ml_conv_new, ffn_conv_new)
    return y.reshape(b, t, d), states


def kernel(x_prompt, x_sample, cache_fox_k, cache_fox_v, cache_fox_logf, state_mlstm_c, state_mlstm_n, state_mlstm_m, state_mlstm_conv, cache_mem_k, cache_mem_v, state_ffn_conv, mem_prompt, norm_mix_pre, w_in, b_in, fox_q_norm, fox_k_norm, mlstm_conv_w, mlstm_conv_b, mlstm_head_norm, norm_mem, w_mem_kv, w_br_a, w_br_b, w_br_m, w_out, norm_mix_post, norm_ffn_pre, w_up, ffn_conv_w, ffn_conv_b, w_down, norm_ffn_post):
    depth = w_in.shape[0]
    hp, hs = x_prompt, x_sample
    b = x_prompt.shape[0]
    new_p = [[] for _ in range(10)]
    new_s = [[] for _ in range(8)]
    for l in range(depth):
        W = _prep_weights(norm_mix_pre[l], w_in[l], b_in[l], fox_q_norm[l], fox_k_norm[l],
                          mlstm_conv_w[l], mlstm_conv_b[l], mlstm_head_norm[l], w_br_a[l], w_br_b[l],
                          w_br_m[l], w_out[l], norm_mix_post[l], norm_ffn_pre[l], w_up[l],
                          ffn_conv_w[l], ffn_conv_b[l], w_down[l], norm_ffn_post[l])
        mlen = mem_prompt.shape[1]
        kv = _norm_matmul(mem_prompt.reshape(b * mlen, D_MODEL), norm_mem[l].reshape(1, -1).astype(F32),
                          w_mem_kv[l].astype(BF16))
        mem_k_p = kv[:, :MEM_W].reshape(b, mlen, MEM_HEADS, MEM_DIM)
        mem_v_p = kv[:, MEM_W:].reshape(b, mlen, MEM_HEADS, MEM_DIM)
        hp, st_p = _layer(
            hp, None,
            jnp.zeros((b, ML_HEADS, ML_DIM, ML_DIM), F32), jnp.zeros((b, ML_HEADS, ML_DIM), F32),
            jnp.zeros((b, ML_HEADS), F32), jnp.zeros((b, ML_CONV - 1, 2 * ML_W), F32),
            mem_k_p, mem_v_p, jnp.zeros((b, FFN_CONV - 1, 2 * D_FF), F32), W)
        hs, st_s = _layer(
            hs, (cache_fox_k[l], cache_fox_v[l], cache_fox_logf[l]),
            state_mlstm_c[l], state_mlstm_n[l], state_mlstm_m[l], state_mlstm_conv[l],
            cache_mem_k[l], cache_mem_v[l], state_ffn_conv[l], W)
        for acc, a in zip(new_p, st_p + (mem_k_p, mem_v_p)):
            acc.append(a)
        for acc, a in zip(new_s, st_s):
            acc.append(a)
    sp = [jnp.stack(a, axis=0) for a in new_p]
    ss = [jnp.stack(a, axis=0) for a in new_s]
    return (hp, hs, sp[0], sp[1], sp[2], sp[3], sp[4], sp[5], sp[6], sp[7], sp[8], sp[9],
            ss[0], ss[1], ss[2], ss[3], ss[4], ss[5], ss[6], ss[7])
```

```python
import functools

import jax
import jax.numpy as jnp
from jax import lax
from jax.experimental import pallas as pl
from jax.experimental.pallas import tpu as pltpu

F32 = jnp.float32
BF16 = jnp.bfloat16

D_MODEL = 1024
FOX_HEADS = 8
FOX_DIM = 64
ML_HEADS = 4
ML_DIM = 128
ML_CONV = 4
MEM_HEADS = 4
MEM_DIM = 128
D_FF = 2816
FFN_CONV = 3
N_BRANCH = 3
EPS = 1e-6
FOX_W = FOX_HEADS * FOX_DIM
ML_W = ML_HEADS * ML_DIM
MEM_W = MEM_HEADS * MEM_DIM

LANES = 128
SUBLANES = 8
NEG_BIG = -1e30
LOG2E = 1.4426950408889634
VMEM_LIMIT = 56 * 1024 * 1024

_C_FQ, _C_FK, _C_FV = 0, 512, 1024
_C_MQK, _C_MV, _C_MO, _C_CQ, _C_SM = 1536, 2560, 3072, 3584, 4096
_IN_COLS = 4224
_IN_GROUPS = ("w_in_a", "w_in_b", "w_in_c", "w_in_s")
_SM_FF, _SM_MI, _SM_MF = 0, 8, 12
_SM_ROWS = 16


def _params(*sem):
    return pltpu.CompilerParams(dimension_semantics=sem, vmem_limit_bytes=VMEM_LIMIT)


def _resident(shape):
    nd = len(shape)
    return pl.BlockSpec(shape, lambda *_: (0,) * nd, pipeline_mode=pl.Buffered(1))


def _rms(x, g):
    return x * lax.rsqrt(jnp.mean(x * x, axis=-1, keepdims=True) + EPS) * g


def _dot(a, b):
    return jnp.dot(a, b, preferred_element_type=F32)


def _dot_nt(a, b):
    return lax.dot_general(a, b, (((1,), (1,)), ((), ())), preferred_element_type=F32)


def _dot_tn(a, b):
    return lax.dot_general(a, b, (((0,), (0,)), ((), ())), preferred_element_type=F32)


def _log_sigmoid(z):
    return jnp.minimum(z, 0.0) - jnp.log1p(jnp.exp(-jnp.abs(z)))


def _sigmoid(z):
    return 1.0 / (1.0 + jnp.exp(-z))


def _gelu_tanh(x):
    c = 0.7978845608028654
    hx = 0.5 * x
    return hx + hx * jnp.tanh(x * (c + (c * 0.044715) * (x * x)))


def _in_proj_kernel(x_ref, g_ref, wa_ref, wb_ref, wc_ref, ws_ref, b_ref, qn_ref, kn_ref, bd_ref,
                    q_ref, kb_ref, vb_ref, fk_ref, fv_ref, mqk_ref, mv_ref, mo_ref, cq_ref, sm_ref, *rest,
                    feature_major):
    if feature_major:
        smt_ref, = rest
    xn = _rms(x_ref[...], g_ref[...]).astype(BF16)

    groups = ((_C_FQ, wa_ref), (_C_MQK, wb_ref), (_C_MO, wc_ref), (_C_SM, ws_ref))

    def proj(lo, hi):
        base, w_ref = [g for g in groups if g[0] <= lo][-1]
        return _dot(xn, w_ref[:, lo - base:hi - base]) + b_ref[:, lo:hi]

    def head_norm(z, gn):
        ms = _dot((z * z).astype(BF16), bd_ref[...])
        return z * lax.rsqrt(ms + EPS) * gn

    def store_heads(dense_ref, state_ref, val):
        dense_ref[...] = val.astype(BF16)
        if feature_major:
            state_ref[0] = val.T
        else:
            for h in range(FOX_HEADS):
                state_ref[:, h, :] = val[:, h * FOX_DIM:(h + 1) * FOX_DIM]

    q_ref[...] = (head_norm(proj(_C_FQ, _C_FK), qn_ref[...]) * (FOX_DIM ** -0.5 * LOG2E)).astype(BF16)
    store_heads(kb_ref, fk_ref, head_norm(proj(_C_FK, _C_FV), kn_ref[...]))
    store_heads(vb_ref, fv_ref, proj(_C_FV, _C_MQK))
    mqk_ref[...] = proj(_C_MQK, _C_MV)
    mv_ref[...] = proj(_C_MV, _C_MO).astype(BF16)
    mo_ref[...] = proj(_C_MO, _C_CQ)
    cq_ref[...] = (proj(_C_CQ, _C_SM) * (MEM_DIM ** -0.5)).astype(BF16)
    z = proj(_C_SM, _IN_COLS)
    lane = lax.broadcasted_iota(jnp.int32, z.shape, 1)
    is_forget = (lane < _SM_MI) | ((lane >= _SM_MF) & (lane < _SM_MF + ML_HEADS))
    sm = jnp.where(is_forget, _log_sigmoid(z), z)
    sm_ref[...] = sm
    if feature_major:
        smt_ref[0] = sm.T[:_SM_ROWS, :]


def _in_proj(x2, W, b, t):
    n = x2.shape[0]
    tm = min(512, n)
    row = lambda w: pl.BlockSpec((tm, w), lambda i: (i, 0))
    feature_major = t % tm == 0
    if feature_major:
        nt = t // tm
        heads = pl.BlockSpec((1, FOX_W, tm), lambda i: (i // nt, 0, i % nt))
        heads_shape = jax.ShapeDtypeStruct((b, FOX_W, t), F32)
    else:
        heads = pl.BlockSpec((tm, FOX_HEADS, FOX_DIM), lambda i: (i, 0, 0))
        heads_shape = jax.ShapeDtypeStruct((n, FOX_HEADS, FOX_DIM), F32)
    outs = [(FOX_W, BF16), (FOX_W, BF16), (FOX_W, BF16), None, None, (2 * ML_W, F32), (ML_W, BF16),
            (ML_W, F32), (MEM_W, BF16), (LANES, F32)]
    out_specs = [heads if o is None else row(o[0]) for o in outs]
    out_shape = [heads_shape if o is None else jax.ShapeDtypeStruct((n, o[0]), o[1]) for o in outs]
    if feature_major:
        out_specs.append(pl.BlockSpec((1, _SM_ROWS, tm), lambda i: (i // nt, 0, i % nt)))
        out_shape.append(jax.ShapeDtypeStruct((b, _SM_ROWS, t), F32))
    res = pl.pallas_call(
        functools.partial(_in_proj_kernel, feature_major=feature_major),
        grid=(n // tm,),
        in_specs=[row(D_MODEL), _resident((1, D_MODEL))] + [_resident(W[k].shape) for k in _IN_GROUPS]
                 + [_resident((1, _IN_COLS)), _resident((1, FOX_W)), _resident((1, FOX_W)),
                    _resident((FOX_W, FOX_W))],
        out_specs=out_specs,
        out_shape=out_shape,
        compiler_params=_params("parallel"),
        name="in_proj",
    )(x2, W["norm_mix_pre"], *[W[k] for k in _IN_GROUPS], W["b_in"], W["fox_q_norm"], W["fox_k_norm"],
      W["head_avg"])
    return tuple(res) if feature_major else tuple(res) + (None,)


def _mem_kv_kernel(x_ref, g_ref, w_ref, k_ref, v_ref):
    kv = _dot(_rms(x_ref[...], g_ref[...]).astype(BF16), w_ref[...])
    rows = x_ref.shape[0]
    for h in range(MEM_HEADS):
        k_ref[pl.ds(h, rows, stride=MEM_HEADS), :] = kv[:, h * MEM_DIM:(h + 1) * MEM_DIM]
        v_ref[pl.ds(h, rows, stride=MEM_HEADS), :] = kv[:, MEM_W + h * MEM_DIM:MEM_W + (h + 1) * MEM_DIM]


def _mem_kv(x2, g, w):
    n, d = x2.shape
    tm = min(512, n)
    out = pl.BlockSpec((tm * MEM_HEADS, MEM_DIM), lambda i: (i, 0))
    shape = jax.ShapeDtypeStruct((n * MEM_HEADS, MEM_DIM), F32)
    return pl.pallas_call(
        _mem_kv_kernel,
        grid=(n // tm,),
        in_specs=[pl.BlockSpec((tm, d), lambda i: (i, 0)), _resident((1, d)), _resident((d, 2 * MEM_W))],
        out_specs=[out, out],
        out_shape=[shape, shape],
        compiler_params=_params("parallel"),
        name="mem_kv",
    )(x2, g, w)


def _cumsum_rows_kernel(f_ref, init_ref, o_ref, carry_ref):
    @pl.when(pl.program_id(0) == 0)
    def _():
        carry_ref[...] = init_ref[...]

    tb = f_ref.shape[-1]
    r = lax.broadcasted_iota(jnp.int32, (tb, tb), 0)
    c = lax.broadcasted_iota(jnp.int32, (tb, tb), 1)
    upper = (r <= c).astype(F32)
    cs = jnp.dot(f_ref[...] * LOG2E, upper, precision=lax.Precision.HIGHEST,
                 preferred_element_type=F32) + carry_ref[:, 0:1]
    o_ref[...] = cs
    carry_ref[...] = jnp.broadcast_to(cs[:, tb - 1:tb], carry_ref.shape)


def _cumsum_rows(f_rows, init):
    r, t = f_rows.shape
    tb = min(256, t)
    return pl.pallas_call(
        _cumsum_rows_kernel,
        grid=(t // tb,),
        in_specs=[pl.BlockSpec((r, tb), lambda j: (0, j)), _resident((r, LANES))],
        out_specs=pl.BlockSpec((r, tb), lambda j: (0, j)),
        out_shape=jax.ShapeDtypeStruct((r, t), F32),
        scratch_shapes=[pltpu.VMEM((r, LANES), F32)],
        compiler_params=_params("arbitrary"),
        name="cumsum_rows",
    )(f_rows, init)


def _stack_heads(q):
    lane = lax.broadcasted_iota(jnp.int32, q.shape, 1)
    zero = jnp.zeros_like(q)
    return jnp.concatenate([jnp.where(lane < FOX_DIM, q, zero), jnp.where(lane >= FOX_DIM, q, zero)], axis=0)


def _unstack_heads(o):
    tq = o.shape[0] // 2
    lane = lax.broadcasted_iota(jnp.int32, (tq, LANES), 1)
    return jnp.where(lane < FOX_DIM, o[:tq], o[tq:])


def _softmax_update(s, bias, mask, m_ref, l_ref):
    tq = s.shape[0] // len(bias)
    tk = s.shape[1]
    parts = [s[i * tq:(i + 1) * tq] + b for i, b in enumerate(bias)]
    s = jnp.concatenate(parts, axis=0) if len(parts) > 1 else parts[0]
    if mask is not None:
        s = jnp.where(mask, s, NEG_BIG)
    tiles = [s[:, i * LANES:(i + 1) * LANES] for i in range(tk // LANES)] if tk >= LANES else [s]
    m_cur = tiles[0]
    for tl in tiles[1:]:
        m_cur = jnp.maximum(m_cur, tl)
    m_prev = m_ref[...]
    m_new = jnp.maximum(m_prev, jnp.max(m_cur, axis=-1, keepdims=True))
    alpha = jnp.exp2(m_prev - m_new)
    if tk >= LANES:
        ps = [jnp.exp2(tl - m_new) for tl in tiles]
        l_ref[...] = alpha * l_ref[...] + sum(ps[1:], ps[0])
        p = jnp.concatenate(ps, axis=1) if len(ps) > 1 else ps[0]
    else:
        p = jnp.exp2(s - m_new[:, :tk])
        l_scaled = alpha * l_ref[...]
        l_ref[...] = l_scaled
        l_ref[:, :tk] = l_scaled[:, :tk] + p
    m_ref[...] = m_new
    return p.astype(BF16), alpha


def _init_stats(m_ref, l_ref, acc_ref):
    m_ref[...] = jnp.full(m_ref.shape, NEG_BIG, F32)
    l_ref[...] = jnp.zeros(l_ref.shape, F32)
    acc_ref[...] = jnp.zeros(acc_ref.shape, F32)


def _causal_stacked(tq):
    rr = lax.broadcasted_iota(jnp.int32, (2 * tq, tq), 0)
    cc = lax.broadcasted_iota(jnp.int32, (2 * tq, tq), 1)
    return cc <= jnp.where(rr >= tq, rr - tq, rr)


def _fox_prompt_kernel(q_ref, kb_ref, vb_ref, cn_ref, o_ref, q2_ref, m_ref, l_ref, acc_ref, s_ref, p_ref,
                       *, tq, nq):
    stages = [(qi, j) for qi in range(nq) for j in range(qi + 1)]
    rows = lambda i: slice(i * tq, (i + 1) * tq)
    causal = _causal_stacked(tq)

    def scores(k):
        qi, j = stages[k]
        if j == 0:
            q2_ref[qi % 2] = _stack_heads(q_ref[0, rows(qi), :])
        s_ref[k % 2] = _dot_nt(q2_ref[qi % 2], kb_ref[0, rows(j), :])

    def finish(qi, acc):
        o = acc / jnp.sum(l_ref[qi % 2], axis=-1, keepdims=True)
        o_ref[0, rows(qi), :] = _unstack_heads(o).astype(BF16)

    scores(0)
    for k, (qi, j) in enumerate(stages):
        st = qi % 2
        if k + 1 < len(stages):
            scores(k + 1)
        if k > 0:
            qi_prev, j_prev = stages[k - 1]
            pv_prev = _dot(p_ref[(k - 1) % 2], vb_ref[0, rows(j_prev), :])
            if qi_prev != qi:
                finish(qi_prev, (acc_ref[qi_prev % 2] + pv_prev) if j_prev > 0 else pv_prev)
        if j == 0:
            m_ref[st] = jnp.full(m_ref.shape[1:], NEG_BIG, F32)
            l_ref[st] = jnp.zeros(l_ref.shape[1:], F32)
        bias = [cn_ref[0, 0, hh:hh + 1, qi * tq:qi * tq + 1] - cn_ref[0, 0, hh:hh + 1, rows(j)] for hh in range(2)]
        p_cur, alpha = _softmax_update(s_ref[k % 2], bias, causal if j == qi else None, m_ref.at[st], l_ref.at[st])
        p_ref[k % 2] = p_cur
        if j > 0:
            acc_ref[st] = alpha * ((acc_ref[st] + pv_prev) if j > 1 else pv_prev)
    qi, j = stages[-1]
    pv_last = _dot(p_ref[(len(stages) - 1) % 2], vb_ref[0, rows(j), :])
    finish(qi, (acc_ref[qi % 2] + pv_last) if j > 0 else pv_last)


def _fox_attn_prompt(q, k_new, v_new, c_new):
    b, t, _ = q.shape
    tq = min(512, t)
    seq = pl.BlockSpec((1, t, LANES), lambda i, p: (i, 0, p))
    stat = pltpu.VMEM((2, 2 * tq, LANES), F32)
    return pl.pallas_call(
        functools.partial(_fox_prompt_kernel, tq=tq, nq=t // tq),
        grid=(b, FOX_HEADS // 2),
        in_specs=[seq, seq, seq, pl.BlockSpec((1, 1, 2, t), lambda i, p: (i, p, 0, 0))],
        out_specs=seq,
        out_shape=jax.ShapeDtypeStruct((b, t, FOX_W), BF16),
        scratch_shapes=[pltpu.VMEM((2, 2 * tq, LANES), BF16), stat, stat, stat,
                        pltpu.VMEM((2, 2 * tq, tq), F32), pltpu.VMEM((2, 2 * tq, tq), BF16)],
        compiler_params=_params("parallel", "parallel"),
        name="fox_attn_prompt",
    )(q, k_new, v_new, c_new)


def _fox_cached_kernel(q_ref, kn_ref, vn_ref, cn_ref, kc_ref, vc_ref, cc_ref, o_ref,
                       qh_ref, m_ref, l_ref, acc_ref, *, t, nblk, tkc):
    j = pl.program_id(1)
    head = lambda h: slice(h * FOX_DIM, (h + 1) * FOX_DIM)

    @pl.when(j == 0)
    def _():
        for h in range(FOX_HEADS):
            qh_ref[h] = q_ref[0, :, head(h)]
        _init_stats(m_ref, l_ref, acc_ref)

    def attend_all(scores, c_rows, mask, weighted_values):
        s_all = [scores(h) for h in range(FOX_HEADS)]
        stats = [_softmax_update(s_all[h], [cn_ref[0, h:h + 1, 0:1] - c_rows[0, h:h + 1, :]], mask,
                                 m_ref.at[h], l_ref.at[h]) for h in range(FOX_HEADS)]
        for h, (p, alpha) in enumerate(stats):
            acc_ref[h] = alpha[:, :FOX_DIM] * acc_ref[h] + weighted_values(h, p)

    attend_all(lambda h: _dot(qh_ref[h], kc_ref[head(h), :].astype(BF16)), cc_ref, None,
               lambda h, p: _dot_nt(p, vc_ref[head(h), :].astype(BF16)))

    @pl.when(j == nblk - 1)
    def _():
        rr = lax.broadcasted_iota(jnp.int32, (t, t), 0)
        cc = lax.broadcasted_iota(jnp.int32, (t, t), 1)
        attend_all(lambda h: _dot_nt(qh_ref[h], kn_ref[0, :, head(h)]), cn_ref, cc <= rr,
                   lambda h, p: _dot(p, vn_ref[0, :, head(h)]))
        for h in range(FOX_HEADS):
            o = acc_ref[h] / jnp.sum(l_ref[h], axis=-1, keepdims=True)
            o_ref[0, :, head(h)] = o.astype(BF16)


def _fox_attn_cached(q, k_new, v_new, c_new, k_cache, v_cache, c_cache):
    b, t, _ = q.shape
    plen = c_cache.shape[-1]
    tkc = min(1024, plen)
    nblk = plen // tkc
    new = pl.BlockSpec((1, t, FOX_W), lambda i, j: (i, 0, 0))
    cache = pl.BlockSpec((FOX_W, tkc), lambda i, j: (i, j))
    return pl.pallas_call(
        functools.partial(_fox_cached_kernel, t=t, nblk=nblk, tkc=tkc),
        grid=(b, nblk),
        in_specs=[new, new, new, pl.BlockSpec((1, FOX_HEADS, t), lambda i, j: (i, 0, 0)),
                  cache, cache, pl.BlockSpec((1, FOX_HEADS, tkc), lambda i, j: (i, 0, j))],
        out_specs=new,
        out_shape=jax.ShapeDtypeStruct((b, t, FOX_W), BF16),
        scratch_shapes=[pltpu.VMEM((FOX_HEADS, t, FOX_DIM), BF16), pltpu.VMEM((FOX_HEADS, t, LANES), F32),
                        pltpu.VMEM((FOX_HEADS, t, LANES), F32), pltpu.VMEM((FOX_HEADS, t, FOX_DIM), F32)],
        compiler_params=_params("parallel", "arbitrary"),
        name="fox_attn_cached",
    )(q, k_new, v_new, c_new, k_cache, v_cache, c_cache)


_XP_HEAD = SUBLANES


def _mlstm_kernel(mqk_ref, mv_ref, mo_ref, sm_ref, smt_ref, cw_ref, cb_ref, hn_ref,
                  conv0_ref, c0_ref, n0_ref, m0_ref,
                  o_ref, conv_out_ref, c_out_ref, n_out_ref, m_out_ref,
                  xp_ref, c_ref, n_ref, m_ref, *, L):
    ci = pl.program_id(1)
    prev = ML_CONV - 1

    @pl.when(ci == 0)
    def _():
        xp_ref[_XP_HEAD - prev:_XP_HEAD, :] = conv0_ref[0]
        c_ref[...] = c0_ref[0]
        n_ref[...] = n0_ref[0]
        m_ref[...] = m0_ref[0]

    xp_ref[_XP_HEAD:_XP_HEAD + L, :] = mqk_ref[...]
    y = cb_ref[...] + sum(xp_ref[_XP_HEAD - prev + j:_XP_HEAD - prev + j + L, :] * cw_ref[j:j + 1, :]
                          for j in range(ML_CONV))
    qk = y * _sigmoid(y)
    tail = xp_ref[_XP_HEAD + L - prev:_XP_HEAD + L, :]
    conv_out_ref[0] = tail
    xp_ref[_XP_HEAD - prev:_XP_HEAD, :] = tail

    rr = lax.broadcasted_iota(jnp.int32, (L, L), 0)
    cc = lax.broadcasted_iota(jnp.int32, (L, L), 1)
    causal = cc <= rr
    sm = sm_ref[...]
    smt = smt_ref[0]
    hi = lax.Precision.HIGHEST
    bt_cols = jnp.dot(causal.astype(F32), sm, precision=hi, preferred_element_type=F32)
    bt_rows = jnp.dot(smt, (rr <= cc).astype(F32), precision=hi, preferred_element_type=F32)

    heads = range(ML_HEADS)
    sl = [slice(h * ML_DIM, (h + 1) * ML_DIM) for h in heads]
    q = [qk[:, sl[h]] * (ML_DIM ** -0.5) for h in heads]
    k = [qk[:, ML_W + h * ML_DIM:ML_W + (h + 1) * ML_DIM] for h in heads]
    qb = [a.astype(BF16) for a in q]
    kb = [a.astype(BF16) for a in k]
    c_p = [c_ref[h] for h in heads]
    n_p = [n_ref[h] for h in heads]
    m_p = [m_ref[h][:, 0:1] for h in heads]
    qk_scores = [_dot_nt(qb[h], kb[h]) for h in heads]
    q_c = [_dot_nt(qb[h], c_p[h].astype(BF16)) for h in heads]

    s_list, w_inter, den, m_t, wg, g_max, b_end = [], [], [], [], [], [], []
    for h in heads:
        it_col = sm[:, _SM_MI + h:_SM_MI + h + 1]
        it_row = smt[_SM_MI + h:_SM_MI + h + 1, :]
        bt_col = bt_cols[:, _SM_MF + h:_SM_MF + h + 1]
        bt_row = bt_rows[_SM_MF + h:_SM_MF + h + 1, :]
        b_end.append(bt_col[L - 1:L, :])
        log_w = bt_col - bt_row + it_row
        m_intra = jnp.max(jnp.where(causal, log_w, NEG_BIG), axis=-1, keepdims=True)
        log_inter = bt_col + m_p[h]
        m_t.append(jnp.maximum(log_inter, m_intra))
        dmat = jnp.where(causal, jnp.exp(log_w - m_t[h]), 0.0)
        s_list.append(qk_scores[h] * dmat)
        w_inter.append(jnp.exp(log_inter - m_t[h]))
        den.append(w_inter[h] * jnp.sum(q[h] * n_p[h], axis=-1, keepdims=True)
                   + jnp.sum(s_list[h], axis=-1, keepdims=True))
        g_col = b_end[h] - bt_col + it_col
        g_max.append(jnp.max(g_col, axis=0, keepdims=True))
        wg.append(jnp.exp(g_col - g_max[h]))

    s_v = [_dot(s_list[h].astype(BF16), mv_ref[:, sl[h]]) for h in heads]
    kv_blk = [_dot_tn((mv_ref[:, sl[h]].astype(F32) * wg[h]).astype(BF16), kb[h]) for h in heads]

    for h in heads:
        num = w_inter[h] * q_c[h] + s_v[h]
        hout = num / jnp.maximum(jnp.abs(den[h]), jnp.exp(-m_t[h]))
        hg = hout * _sigmoid(mo_ref[:, sl[h]])
        o_ref[:, sl[h]] = _rms(hg, hn_ref[:, sl[h]]).astype(BF16)

        k_blk = jnp.sum(k[h] * wg[h], axis=0, keepdims=True)
        m_new = jnp.maximum(b_end[h] + m_p[h], g_max[h])
        decay = jnp.exp(b_end[h] + m_p[h] - m_new)
        scale = jnp.exp(g_max[h] - m_new)
        c_new = decay * c_p[h] + scale * kv_blk[h]
        n_new = decay * n_p[h] + scale * k_blk
        c_ref[h] = c_new
        n_ref[h] = n_new
        m_ref[h] = jnp.broadcast_to(m_new, (1, LANES))
        c_out_ref[0, h] = c_new
        n_out_ref[0, h] = n_new
        m_out_ref[0, h] = jnp.broadcast_to(m_new, (1, LANES))


def _mlstm(mqk, mv, mo, sm, smt, W, conv0, c0, n0, m0, b, t):
    L = min(256, t)
    nc = t // L
    row = lambda w: pl.BlockSpec((L, w), lambda i, j: (i * nc + j, 0))
    st = lambda *shape: pl.BlockSpec((1,) + shape, lambda i, j: (i,) + (0,) * len(shape))
    return pl.pallas_call(
        functools.partial(_mlstm_kernel, L=L),
        grid=(b, nc),
        in_specs=[row(2 * ML_W), row(ML_W), row(ML_W), row(LANES),
                  pl.BlockSpec((1, 16, L), lambda i, j: (i, 0, j)),
                  _resident((ML_CONV, 2 * ML_W)), _resident((1, 2 * ML_W)), _resident((1, ML_W)),
                  st(ML_CONV - 1, 2 * ML_W), st(ML_HEADS, ML_DIM, ML_DIM),
                  st(ML_HEADS, 1, ML_DIM), st(ML_HEADS, 1, LANES)],
        out_specs=[row(ML_W), st(ML_CONV - 1, 2 * ML_W), st(ML_HEADS, ML_DIM, ML_DIM),
                   st(ML_HEADS, 1, ML_DIM), st(ML_HEADS, 1, LANES)],
        out_shape=[jax.ShapeDtypeStruct((b * t, ML_W), BF16),
                   jax.ShapeDtypeStruct((b, ML_CONV - 1, 2 * ML_W), F32),
                   jax.ShapeDtypeStruct((b, ML_HEADS, ML_DIM, ML_DIM), F32),
                   jax.ShapeDtypeStruct((b, ML_HEADS, 1, ML_DIM), F32),
                   jax.ShapeDtypeStruct((b, ML_HEADS, 1, LANES), F32)],
        scratch_shapes=[pltpu.VMEM((_XP_HEAD + L, 2 * ML_W), F32),
                        pltpu.VMEM((ML_HEADS, ML_DIM, ML_DIM), F32),
                        pltpu.VMEM((ML_HEADS, 1, ML_DIM), F32),
                        pltpu.VMEM((ML_HEADS, 1, LANES), F32)],
        compiler_params=_params("parallel", "arbitrary"),
        name="mlstm",
    )(mqk, mv, mo, sm, smt, W["mlstm_conv_w"], W["mlstm_conv_b"], W["mlstm_head_norm"],
      conv0, c0, n0, m0)


def _mem_attn_kernel(q_ref, k_ref, v_ref, o_ref):
    sl = [slice(h * MEM_DIM, (h + 1) * MEM_DIM) for h in range(MEM_HEADS)]
    mlen = k_ref.shape[0] // MEM_HEADS
    head = lambda ref, h: ref[pl.ds(h, mlen, stride=MEM_HEADS), :].astype(BF16)
    s = [_dot_nt(q_ref[:, c], head(k_ref, h)) for h, c in enumerate(sl)]
    p = [jnp.exp(a - jnp.max(a, axis=-1, keepdims=True)) for a in s]
    pv = [_dot(a.astype(BF16), head(v_ref, h)) for h, a in enumerate(p)]
    for a, o, c in zip(p, pv, sl):
        o_ref[:, c] = (o / jnp.sum(a, axis=-1, keepdims=True)).astype(BF16)


def _mem_attn(cq, mem_k, mem_v, b, t):
    tm = min(512, t)
    nt = t // tm
    mlen = mem_k.shape[0] // (b * MEM_HEADS)
    kv = pl.BlockSpec((mlen * MEM_HEADS, MEM_DIM), lambda i: (i // nt, 0))
    return pl.pallas_call(
        _mem_attn_kernel,
        grid=(b * nt,),
        in_specs=[pl.BlockSpec((tm, MEM_W), lambda i: (i, 0)), kv, kv],
        out_specs=pl.BlockSpec((tm, MEM_W), lambda i: (i, 0)),
        out_shape=jax.ShapeDtypeStruct((b * t, MEM_W), BF16),
        compiler_params=_params("parallel"),
        name="mem_attn",
    )(cq, mem_k, mem_v)


def _merge_kernel(x_ref, a_ref, b_ref, m_ref, gpre_ref, wg_ref, bg_ref, wa_ref, wb_ref, wm_ref,
                  wo_ref, gpost_ref, o_ref):
    x = x_ref[...]
    xn = _rms(x, gpre_ref[...]).astype(BF16)
    merged = None
    for i, (br_ref, w_ref) in enumerate(((a_ref, wa_ref), (b_ref, wb_ref), (m_ref, wm_ref))):
        sl = slice(i * D_MODEL, (i + 1) * D_MODEL)
        gate = _sigmoid(_dot(xn, wg_ref[:, sl]) + bg_ref[:, sl])
        term = gate * _dot(br_ref[...], w_ref[...])
        merged = term if merged is None else merged + term
    o_ref[...] = x + _rms(_dot(merged.astype(BF16), wo_ref[...]), gpost_ref[...])


def _merge(x2, a, b, m, W):
    n = x2.shape[0]
    tm = min(512, n)
    row = lambda w: pl.BlockSpec((tm, w), lambda i: (i, 0))
    return pl.pallas_call(
        _merge_kernel,
        grid=(n // tm,),
        in_specs=[row(D_MODEL), row(FOX_W), row(ML_W), row(MEM_W), _resident((1, D_MODEL)),
                  _resident((D_MODEL, N_BRANCH * D_MODEL)), _resident((1, N_BRANCH * D_MODEL)),
                  _resident((FOX_W, D_MODEL)), _resident((ML_W, D_MODEL)), _resident((MEM_W, D_MODEL)),
                  _resident((D_MODEL, D_MODEL)), _resident((1, D_MODEL))],
        out_specs=row(D_MODEL),
        out_shape=jax.ShapeDtypeStruct((n, D_MODEL), F32),
        compiler_params=_params("parallel"),
        name="merge",
    )(x2, a, b, m, W["norm_mix_pre"], W["w_gate"], W["b_gate"], W["w_br_a"], W["w_br_b"],
      W["w_br_m"], W["w_out"], W["norm_mix_post"])


_FF_CHUNK = 512


def _ffn_kernel(x_ref, gpre_ref, wup_ref, cw_ref, cb_ref, wdn_ref, gpost_ref, conv0_ref,
                o_ref, conv_out_ref, buf_ref, carry_ref, hid_ref, *, tm):
    ti = pl.program_id(1)
    prev = FFN_CONV - 1

    @pl.when(ti == 0)
    def _():
        carry_ref[SUBLANES - prev:SUBLANES, :] = conv0_ref[0]

    x = x_ref[...]
    xn = _rms(x, gpre_ref[...]).astype(BF16)

    def conv_cols(lo, w):
        up = _dot(xn, wup_ref[:, lo:lo + w])
        buf_ref[SUBLANES - prev:SUBLANES, 0:w] = carry_ref[SUBLANES - prev:SUBLANES, lo:lo + w]
        buf_ref[SUBLANES:SUBLANES + tm, 0:w] = up
        tail = up[tm - prev:tm, :]
        carry_ref[SUBLANES - prev:SUBLANES, lo:lo + w] = tail
        conv_out_ref[0, :, lo:lo + w] = tail
        return cb_ref[:, lo:lo + w] + sum(
            buf_ref[SUBLANES - prev + j:SUBLANES - prev + j + tm, 0:w] * cw_ref[j:j + 1, lo:lo + w]
            for j in range(FFN_CONV))

    for lo in range(0, D_FF, _FF_CHUNK):
        w = min(_FF_CHUNK, D_FF - lo)
        ua = conv_cols(lo, w)
        ub = conv_cols(D_FF + lo, w)
        hid_ref[:, lo:lo + w] = (_gelu_tanh(ua) * ub).astype(BF16)
    o_ref[...] = x + _rms(_dot(hid_ref[...], wdn_ref[...]), gpost_ref[...])


def _ffn(x1, conv0, W, b, t):
    tm = min(512, t)
    nt = t // tm
    row = pl.BlockSpec((tm, D_MODEL), lambda i, j: (i * nt + j, 0))
    st = pl.BlockSpec((1, FFN_CONV - 1, 2 * D_FF), lambda i, j: (i, 0, 0))
    return pl.pallas_call(
        functools.partial(_ffn_kernel, tm=tm),
        grid=(b, nt),
        in_specs=[row, _resident((1, D_MODEL)), _resident((D_MODEL, 2 * D_FF)),
                  _resident((FFN_CONV, 2 * D_FF)), _resident((1, 2 * D_FF)),
                  _resident((D_FF, D_MODEL)), _resident((1, D_MODEL)), st],
        out_specs=[row, st],
        out_shape=[jax.ShapeDtypeStruct((b * t, D_MODEL), F32),
                   jax.ShapeDtypeStruct((b, FFN_CONV - 1, 2 * D_FF), F32)],
        scratch_shapes=[pltpu.VMEM((SUBLANES + tm, _FF_CHUNK), F32),
                        pltpu.VMEM((SUBLANES, 2 * D_FF), F32), pltpu.VMEM((tm, D_FF), BF16)],
        compiler_params=_params("parallel", "arbitrary"),
        name="ffn",
    )(x1, W["norm_ffn_pre"], W["w_up"], W["ffn_conv_w"], W["ffn_conv_b"], W["w_down"],
      W["norm_ffn_post"], conv0)


def _prep_weights(norm_mix_pre, w_in, b_in, fox_q_norm, fox_k_norm, mlstm_conv_w, mlstm_conv_b,
                  mlstm_head_norm, w_br_a, w_br_b, w_br_m, w_out, norm_mix_post, norm_ffn_pre,
                  w_up, ffn_conv_w, ffn_conv_b, w_down, norm_ffn_post):
    o_ff, o_mq, o_mi, o_mf, o_mo, o_cq, o_g = 1536, 1544, 3080, 3084, 3088, 3600, 4112

    def pack(a):
        pad = jnp.zeros(a.shape[:-1] + (LANES - FOX_HEADS - 2 * ML_HEADS,), a.dtype)
        small = jnp.concatenate([a[..., o_ff:o_mq], a[..., o_mi:o_mf], a[..., o_mf:o_mo], pad], -1)
        return jnp.concatenate([a[..., :o_ff], a[..., o_mq:o_mi], a[..., o_mo:o_g], small], -1)

    row = lambda v: v.reshape(1, -1).astype(F32)
    head = jnp.arange(FOX_W) // FOX_DIM
    return {
        "norm_mix_pre": row(norm_mix_pre),
        "w_in_a": w_in[:, :o_ff].astype(BF16),
        "w_in_b": w_in[:, o_mq:o_mi].astype(BF16),
        "w_in_c": w_in[:, o_mo:o_g].astype(BF16),
        "w_in_s": jnp.concatenate([w_in[:, o_ff:o_mq], w_in[:, o_mi:o_mo],
                                   jnp.zeros((D_MODEL, LANES - FOX_HEADS - 2 * ML_HEADS), F32)], -1).astype(BF16),
        "b_in": row(pack(b_in)),
        "w_gate": w_in[:, o_g:].astype(BF16),
        "b_gate": row(b_in[o_g:]),
        "fox_q_norm": row(jnp.tile(fox_q_norm, FOX_HEADS)),
        "fox_k_norm": row(jnp.tile(fox_k_norm, FOX_HEADS)),
        "head_avg": ((head[:, None] == head[None, :]).astype(F32) / FOX_DIM).astype(BF16),
        "mlstm_conv_w": mlstm_conv_w.astype(F32),
        "mlstm_conv_b": row(mlstm_conv_b),
        "mlstm_head_norm": row(mlstm_head_norm),
        "w_br_a": w_br_a.astype(BF16), "w_br_b": w_br_b.astype(BF16), "w_br_m": w_br_m.astype(BF16),
        "w_out": w_out.astype(BF16),
        "norm_mix_post": row(norm_mix_post),
        "norm_ffn_pre": row(norm_ffn_pre),
        "w_up": w_up.astype(BF16),
        "ffn_conv_w": ffn_conv_w.astype(F32),
        "ffn_conv_b": row(ffn_conv_b),
        "w_down": w_down.astype(BF16),
        "norm_ffn_post": row(norm_ffn_post),
    }


def _rows_layout(a, b, t, r):
    return jnp.transpose(a[:, :r].reshape(b, t, r), (0, 2, 1))


def _layer(x, fox_cache, ml_c0, ml_n0, ml_m0, ml_conv_prev, mem_k, mem_v, ffn_conv_prev, W):
    b, t, d = x.shape
    n = b * t
    x2 = x.reshape(n, d)
    q, kb, vb, fk, fv, mqk, mv, mo, cq, sm, smt = _in_proj(x2, W, b, t)
    if smt is None:
        smt = _rows_layout(sm, b, t, _SM_ROWS)
        f_log = sm[:, :FOX_HEADS].reshape(b, t, FOX_HEADS)
    else:
        fk, fv = (jnp.transpose(a.reshape(b, FOX_HEADS, FOX_DIM, t), (0, 3, 1, 2)) for a in (fk, fv))
        f_log = jnp.transpose(smt[:, :FOX_HEADS, :], (0, 2, 1))

    pairs = FOX_HEADS // 2
    f_rows = smt[:, :FOX_HEADS, :].reshape(b * FOX_HEADS, t)
    zeros_init = jnp.zeros((b * FOX_HEADS, LANES), F32)
    q3, fk3, fv3 = q.reshape(b, t, FOX_W), kb.reshape(b, t, FOX_W), vb.reshape(b, t, FOX_W)
    if fox_cache is None:
        c_new = _cumsum_rows(f_rows, zeros_init)
        a_out = _fox_attn_prompt(q3, fk3, fv3, c_new.reshape(b, pairs, 2, t))
    else:
        k_cache, v_cache, logf_cache = fox_cache
        plen = k_cache.shape[1]
        logf_rows = jnp.transpose(logf_cache.astype(F32), (0, 2, 1)).reshape(b * FOX_HEADS, plen)
        c_cache = _cumsum_rows(logf_rows, zeros_init)
        c_new = _cumsum_rows(f_rows, jnp.broadcast_to(c_cache[:, plen - 1:plen], (b * FOX_HEADS, LANES)))
        feature_major = lambda a: jnp.transpose(a.astype(F32), (0, 2, 3, 1)).reshape(b * FOX_W, plen)
        a_out = _fox_attn_cached(q3, fk3, fv3, c_new.reshape(b, FOX_HEADS, t), feature_major(k_cache),
                                 feature_major(v_cache), c_cache.reshape(b, FOX_HEADS, plen))

    b_out, ml_conv_new, c_t, n_t, m_t = _mlstm(
        mqk, mv, mo, sm, smt, W, ml_conv_prev.astype(F32), ml_c0.astype(F32),
        ml_n0.astype(F32).reshape(b, ML_HEADS, 1, ML_DIM),
        jnp.broadcast_to(ml_m0.astype(F32)[:, :, None, None], (b, ML_HEADS, 1, LANES)), b, t)

    m_out = _mem_attn(cq, mem_k.astype(F32).reshape(-1, MEM_DIM), mem_v.astype(F32).reshape(-1, MEM_DIM), b, t)

    x1 = _merge(x2, a_out.reshape(n, FOX_W), b_out, m_out, W)
    y, ffn_conv_new = _ffn(x1, ffn_conv_prev.astype(F32), W, b, t)

    states = (fk.reshape(b, t, FOX_HEADS, FOX_DIM), fv.reshape(b, t, FOX_HEADS, FOX_DIM),
              f_log, c_t, n_t.reshape(b, ML_HEADS, ML_DIM),
              m_t[:, :, 0, 0], ml_conv_new, ffn_conv_new)
    return y.reshape(b, t, d), states


def kernel(x_prompt, x_sample, cache_fox_k, cache_fox_v, cache_fox_logf, state_mlstm_c, state_mlstm_n, state_mlstm_m, state_mlstm_conv, cache_mem_k, cache_mem_v, state_ffn_conv, mem_prompt, norm_mix_pre, w_in, b_in, fox_q_norm, fox_k_norm, mlstm_conv_w, mlstm_conv_b, mlstm_head_norm, norm_mem, w_mem_kv, w_br_a, w_br_b, w_br_m, w_out, norm_mix_post, norm_ffn_pre, w_up, ffn_conv_w, ffn_conv_b, w_down, norm_ffn_post):
    depth = w_in.shape[0]
    hp, hs = x_prompt, x_sample
    b = x_prompt.shape[0]
    new_p = [[] for _ in range(10)]
    new_s = [[] for _ in range(8)]
    for l in range(depth):
        W = _prep_weights(norm_mix_pre[l], w_in[l], b_in[l], fox_q_norm[l], fox_k_norm[l],
                          mlstm_conv_w[l], mlstm_conv_b[l], mlstm_head_norm[l], w_br_a[l], w_br_b[l],
                          w_br_m[l], w_out[l], norm_mix_post[l], norm_ffn_pre[l], w_up[l],
                          ffn_conv_w[l], ffn_conv_b[l], w_down[l], norm_ffn_post[l])
        mlen = mem_prompt.shape[1]
        mem_k_p, mem_v_p = (a.reshape(b, mlen, MEM_HEADS, MEM_DIM) for a in _mem_kv(
            mem_prompt.reshape(b * mlen, D_MODEL), norm_mem[l].reshape(1, -1).astype(F32),
            w_mem_kv[l].astype(BF16)))
        hp, st_p = _layer(
            hp, None,
            jnp.zeros((b, ML_HEADS, ML_DIM, ML_DIM), F32), jnp.zeros((b, ML_HEADS, ML_DIM), F32),
            jnp.zeros((b, ML_HEADS), F32), jnp.zeros((b, ML_CONV - 1, 2 * ML_W), F32),
            mem_k_p, mem_v_p, jnp.zeros((b, FFN_CONV - 1, 2 * D_FF), F32), W)
        hs, st_s = _layer(
            hs, (cache_fox_k[l], cache_fox_v[l], cache_fox_logf[l]),
            state_mlstm_c[l], state_mlstm_n[l], state_mlstm_m[l], state_mlstm_conv[l],
            cache_mem_k[l], cache_mem_v[l], state_ffn_conv[l], W)
        for acc, a in zip(new_p, st_p + (mem_k_p, mem_v_p)):
            acc.append(a)
        for acc, a in zip(new_s, st_s):
            acc.append(a)
    sp = [jnp.stack(a, axis=0) for a in new_p]
    ss = [jnp.stack(a, axis=0) for a in new_s]
    return (hp, hs, sp[0], sp[1], sp[2], sp[3], sp[4], sp[5], sp[6], sp[7], sp[8], sp[9],
            ss[0], ss[1], ss[2], ss[3], ss[4], ss[5], ss[6], ss[7])
```

```python
import functools

import jax
import jax.numpy as jnp
from jax import lax
from jax.experimental import pallas as pl
from jax.experimental.pallas import tpu as pltpu

F32 = jnp.float32
BF16 = jnp.bfloat16

D_MODEL = 1024
FOX_HEADS = 8
FOX_DIM = 64
ML_HEADS = 4
ML_DIM = 128
ML_CONV = 4
MEM_HEADS = 4
MEM_DIM = 128
D_FF = 2816
FFN_CONV = 3
N_BRANCH = 3
EPS = 1e-6
FOX_W = FOX_HEADS * FOX_DIM
ML_W = ML_HEADS * ML_DIM
MEM_W = MEM_HEADS * MEM_DIM

LANES = 128
SUBLANES = 8
NEG_BIG = -1e30
LOG2E = 1.4426950408889634
VMEM_LIMIT = 56 * 1024 * 1024

_C_FQ, _C_FK, _C_FV = 0, 512, 1024
_C_MQK, _C_MV, _C_MO, _C_CQ, _C_SM = 1536, 2560, 3072, 3584, 4096
_IN_COLS = 4224
_IN_GROUPS = ("w_in_a", "w_in_b", "w_in_c", "w_in_s")
_SM_FF, _SM_MI, _SM_MF = 0, 8, 12
_SM_ROWS = 16


def _params(*sem):
    return pltpu.CompilerParams(dimension_semantics=sem, vmem_limit_bytes=VMEM_LIMIT)


def _resident(shape):
    nd = len(shape)
    return pl.BlockSpec(shape, lambda *_: (0,) * nd, pipeline_mode=pl.Buffered(1))


def _rms(x, g):
    return x * lax.rsqrt(jnp.mean(x * x, axis=-1, keepdims=True) + EPS) * g


def _dot(a, b):
    return jnp.dot(a, b, preferred_element_type=F32)


def _dot_nt(a, b):
    return lax.dot_general(a, b, (((1,), (1,)), ((), ())), preferred_element_type=F32)


def _dot_tn(a, b):
    return lax.dot_general(a, b, (((0,), (0,)), ((), ())), preferred_element_type=F32)


def _log_sigmoid(z):
    return jnp.minimum(z, 0.0) - jnp.log1p(jnp.exp(-jnp.abs(z)))


def _sigmoid(z):
    return 1.0 / (1.0 + jnp.exp(-z))


def _gelu_tanh(x):
    c = 0.7978845608028654
    hx = 0.5 * x
    return hx + hx * jnp.tanh(x * (c + (c * 0.044715) * (x * x)))


def _in_proj_kernel(x_ref, g_ref, wa_ref, wb_ref, wc_ref, ws_ref, b_ref, qn_ref, kn_ref, bd_ref,
                    q_ref, kb_ref, vb_ref, fk_ref, fv_ref, mqk_ref, mv_ref, mo_ref, cq_ref, sm_ref, *rest,
                    feature_major):
    if feature_major:
        smt_ref, = rest
    xn = _rms(x_ref[...], g_ref[...]).astype(BF16)

    groups = ((_C_FQ, wa_ref), (_C_MQK, wb_ref), (_C_MO, wc_ref), (_C_SM, ws_ref))

    def proj(lo, hi):
        base, w_ref = [g for g in groups if g[0] <= lo][-1]
        return _dot(xn, w_ref[:, lo - base:hi - base]) + b_ref[:, lo:hi]

    def head_norm(z, gn):
        ms = _dot((z * z).astype(BF16), bd_ref[...])
        return z * lax.rsqrt(ms + EPS) * gn

    def store_heads(dense_ref, state_ref, val):
        dense_ref[...] = val.astype(BF16)
        if feature_major:
            state_ref[0] = val.T
        else:
            for h in range(FOX_HEADS):
                state_ref[:, h, :] = val[:, h * FOX_DIM:(h + 1) * FOX_DIM]

    q_ref[...] = (head_norm(proj(_C_FQ, _C_FK), qn_ref[...]) * (FOX_DIM ** -0.5 * LOG2E)).astype(BF16)
    store_heads(kb_ref, fk_ref, head_norm(proj(_C_FK, _C_FV), kn_ref[...]))
    store_heads(vb_ref, fv_ref, proj(_C_FV, _C_MQK))
    mqk_ref[...] = proj(_C_MQK, _C_MV)
    mv_ref[...] = proj(_C_MV, _C_MO).astype(BF16)
    mo_ref[...] = proj(_C_MO, _C_CQ)
    cq_ref[...] = (proj(_C_CQ, _C_SM) * (MEM_DIM ** -0.5)).astype(BF16)
    z = proj(_C_SM, _IN_COLS)
    lane = lax.broadcasted_iota(jnp.int32, z.shape, 1)
    is_forget = (lane < _SM_MI) | ((lane >= _SM_MF) & (lane < _SM_MF + ML_HEADS))
    sm = jnp.where(is_forget, _log_sigmoid(z), z)
    sm_ref[...] = sm
    if feature_major:
        smt_ref[0] = sm.T[:_SM_ROWS, :]


def _in_proj(x2, W, b, t):
    n = x2.shape[0]
    tm = min(512, n)
    row = lambda w: pl.BlockSpec((tm, w), lambda i: (i, 0))
    feature_major = t % tm == 0
    if feature_major:
        nt = t // tm
        heads = pl.BlockSpec((1, FOX_W, tm), lambda i: (i // nt, 0, i % nt))
        heads_shape = jax.ShapeDtypeStruct((b, FOX_W, t), F32)
    else:
        heads = pl.BlockSpec((tm, FOX_HEADS, FOX_DIM), lambda i: (i, 0, 0))
        heads_shape = jax.ShapeDtypeStruct((n, FOX_HEADS, FOX_DIM), F32)
    outs = [(FOX_W, BF16), (FOX_W, BF16), (FOX_W, BF16), None, None, (2 * ML_W, F32), (ML_W, BF16),
            (ML_W, F32), (MEM_W, BF16), (LANES, F32)]
    out_specs = [heads if o is None else row(o[0]) for o in outs]
    out_shape = [heads_shape if o is None else jax.ShapeDtypeStruct((n, o[0]), o[1]) for o in outs]
    if feature_major:
        out_specs.append(pl.BlockSpec((1, _SM_ROWS, tm), lambda i: (i // nt, 0, i % nt)))
        out_shape.append(jax.ShapeDtypeStruct((b, _SM_ROWS, t), F32))
    res = pl.pallas_call(
        functools.partial(_in_proj_kernel, feature_major=feature_major),
        grid=(n // tm,),
        in_specs=[row(D_MODEL), _resident((1, D_MODEL))] + [_resident(W[k].shape) for k in _IN_GROUPS]
                 + [_resident((1, _IN_COLS)), _resident((1, FOX_W)), _resident((1, FOX_W)),
                    _resident((FOX_W, FOX_W))],
        out_specs=out_specs,
        out_shape=out_shape,
        compiler_params=_params("parallel"),
        name="in_proj",
    )(x2, W["norm_mix_pre"], *[W[k] for k in _IN_GROUPS], W["b_in"], W["fox_q_norm"], W["fox_k_norm"],
      W["head_avg"])
    return tuple(res) if feature_major else tuple(res) + (None,)


def _mem_kv_kernel(x_ref, g_ref, w_ref, k_ref, v_ref):
    kv = _dot(_rms(x_ref[...], g_ref[...]).astype(BF16), w_ref[...])
    rows = x_ref.shape[0]
    for h in range(MEM_HEADS):
        k_ref[pl.ds(h, rows, stride=MEM_HEADS), :] = kv[:, h * MEM_DIM:(h + 1) * MEM_DIM]
        v_ref[pl.ds(h, rows, stride=MEM_HEADS), :] = kv[:, MEM_W + h * MEM_DIM:MEM_W + (h + 1) * MEM_DIM]


def _mem_kv(x2, g, w):
    n, d = x2.shape
    tm = min(512, n)
    out = pl.BlockSpec((tm * MEM_HEADS, MEM_DIM), lambda i: (i, 0))
    shape = jax.ShapeDtypeStruct((n * MEM_HEADS, MEM_DIM), F32)
    return pl.pallas_call(
        _mem_kv_kernel,
        grid=(n // tm,),
        in_specs=[pl.BlockSpec((tm, d), lambda i: (i, 0)), _resident((1, d)), _resident((d, 2 * MEM_W))],
        out_specs=[out, out],
        out_shape=[shape, shape],
        compiler_params=_params("parallel"),
        name="mem_kv",
    )(x2, g, w)


def _cumsum_rows_kernel(f_ref, init_ref, o_ref, carry_ref):
    @pl.when(pl.program_id(0) == 0)
    def _():
        carry_ref[...] = init_ref[...]

    tb = f_ref.shape[-1]
    r = lax.broadcasted_iota(jnp.int32, (tb, tb), 0)
    c = lax.broadcasted_iota(jnp.int32, (tb, tb), 1)
    upper = (r <= c).astype(F32)
    cs = jnp.dot(f_ref[...] * LOG2E, upper, precision=lax.Precision.HIGHEST,
                 preferred_element_type=F32) + carry_ref[:, 0:1]
    o_ref[...] = cs
    carry_ref[...] = jnp.broadcast_to(cs[:, tb - 1:tb], carry_ref.shape)


def _cumsum_rows(f_rows, init):
    r, t = f_rows.shape
    tb = min(1024, t)
    return pl.pallas_call(
        _cumsum_rows_kernel,
        grid=(t // tb,),
        in_specs=[pl.BlockSpec((r, tb), lambda j: (0, j)), _resident((r, LANES))],
        out_specs=pl.BlockSpec((r, tb), lambda j: (0, j)),
        out_shape=jax.ShapeDtypeStruct((r, t), F32),
        scratch_shapes=[pltpu.VMEM((r, LANES), F32)],
        compiler_params=_params("arbitrary"),
        name="cumsum_rows",
    )(f_rows, init)


def _stack_heads(q):
    lane = lax.broadcasted_iota(jnp.int32, q.shape, 1)
    zero = jnp.zeros_like(q)
    return jnp.concatenate([jnp.where(lane < FOX_DIM, q, zero), jnp.where(lane >= FOX_DIM, q, zero)], axis=0)


def _unstack_heads(o):
    tq = o.shape[0] // 2
    lane = lax.broadcasted_iota(jnp.int32, (tq, LANES), 1)
    return jnp.where(lane < FOX_DIM, o[:tq], o[tq:])


def _softmax_update(s, bias, mask, m_ref, l_ref):
    tq = s.shape[0] // len(bias)
    tk = s.shape[1]
    parts = [s[i * tq:(i + 1) * tq] + b for i, b in enumerate(bias)]
    s = jnp.concatenate(parts, axis=0) if len(parts) > 1 else parts[0]
    if mask is not None:
        s = jnp.where(mask, s, NEG_BIG)
    tiles = [s[:, i * LANES:(i + 1) * LANES] for i in range(tk // LANES)] if tk >= LANES else [s]
    m_cur = tiles[0]
    for tl in tiles[1:]:
        m_cur = jnp.maximum(m_cur, tl)
    m_prev = m_ref[...]
    m_new = jnp.maximum(m_prev, jnp.max(m_cur, axis=-1, keepdims=True))
    alpha = jnp.exp2(m_prev - m_new)
    if tk >= LANES:
        ps = [jnp.exp2(tl - m_new) for tl in tiles]
        l_ref[...] = alpha * l_ref[...] + sum(ps[1:], ps[0])
        p = jnp.concatenate(ps, axis=1) if len(ps) > 1 else ps[0]
    else:
        p = jnp.exp2(s - m_new[:, :tk])
        l_scaled = alpha * l_ref[...]
        l_ref[...] = l_scaled
        l_ref[:, :tk] = l_scaled[:, :tk] + p
    m_ref[...] = m_new
    return p.astype(BF16), alpha


def _init_stats(m_ref, l_ref, acc_ref):
    m_ref[...] = jnp.full(m_ref.shape, NEG_BIG, F32)
    l_ref[...] = jnp.zeros(l_ref.shape, F32)
    acc_ref[...] = jnp.zeros(acc_ref.shape, F32)


def _causal_stacked(tq):
    rr = lax.broadcasted_iota(jnp.int32, (2 * tq, tq), 0)
    cc = lax.broadcasted_iota(jnp.int32, (2 * tq, tq), 1)
    return cc <= jnp.where(rr >= tq, rr - tq, rr)


def _fox_prompt_kernel(q_ref, kb_ref, vb_ref, cn_ref, o_ref, q2_ref, m_ref, l_ref, acc_ref, s_ref, p_ref,
                       *, tq, nq):
    stages = [(qi, j) for qi in range(nq) for j in range(qi + 1)]
    rows = lambda i: slice(i * tq, (i + 1) * tq)
    causal = _causal_stacked(tq)

    def scores(k):
        qi, j = stages[k]
        if j == 0:
            q2_ref[qi % 2] = _stack_heads(q_ref[0, rows(qi), :])
        s_ref[k % 2] = _dot_nt(q2_ref[qi % 2], kb_ref[0, rows(j), :])

    def finish(qi, acc):
        o = acc / jnp.sum(l_ref[qi % 2], axis=-1, keepdims=True)
        o_ref[0, rows(qi), :] = _unstack_heads(o).astype(BF16)

    scores(0)
    for k, (qi, j) in enumerate(stages):
        st = qi % 2
        if k + 1 < len(stages):
            scores(k + 1)
        if k > 0:
            qi_prev, j_prev = stages[k - 1]
            pv_prev = _dot(p_ref[(k - 1) % 2], vb_ref[0, rows(j_prev), :])
            if qi_prev != qi:
                finish(qi_prev, (acc_ref[qi_prev % 2] + pv_prev) if j_prev > 0 else pv_prev)
        if j == 0:
            m_ref[st] = jnp.full(m_ref.shape[1:], NEG_BIG, F32)
            l_ref[st] = jnp.zeros(l_ref.shape[1:], F32)
        bias = [cn_ref[0, 0, hh:hh + 1, qi * tq:qi * tq + 1] - cn_ref[0, 0, hh:hh + 1, rows(j)] for hh in range(2)]
        p_cur, alpha = _softmax_update(s_ref[k % 2], bias, causal if j == qi else None, m_ref.at[st], l_ref.at[st])
        p_ref[k % 2] = p_cur
        if j > 0:
            acc_ref[st] = alpha * ((acc_ref[st] + pv_prev) if j > 1 else pv_prev)
    qi, j = stages[-1]
    pv_last = _dot(p_ref[(len(stages) - 1) % 2], vb_ref[0, rows(j), :])
    finish(qi, (acc_ref[qi % 2] + pv_last) if j > 0 else pv_last)


def _fox_attn_prompt(q, k_new, v_new, c_new):
    b, t, _ = q.shape
    tq = min(512, t)
    seq = pl.BlockSpec((1, t, LANES), lambda i, p: (i, 0, p))
    stat = pltpu.VMEM((2, 2 * tq, LANES), F32)
    return pl.pallas_call(
        functools.partial(_fox_prompt_kernel, tq=tq, nq=t // tq),
        grid=(b, FOX_HEADS // 2),
        in_specs=[seq, seq, seq, pl.BlockSpec((1, 1, 2, t), lambda i, p: (i, p, 0, 0))],
        out_specs=seq,
        out_shape=jax.ShapeDtypeStruct((b, t, FOX_W), BF16),
        scratch_shapes=[pltpu.VMEM((2, 2 * tq, LANES), BF16), stat, stat, stat,
                        pltpu.VMEM((2, 2 * tq, tq), F32), pltpu.VMEM((2, 2 * tq, tq), BF16)],
        compiler_params=_params("parallel", "parallel"),
        name="fox_attn_prompt",
    )(q, k_new, v_new, c_new)


def _fox_cached_kernel(q_ref, kn_ref, vn_ref, cn_ref, kc_ref, vc_ref, cc_ref, o_ref,
                       qh_ref, m_ref, l_ref, acc_ref, *, t, nblk, tkc):
    j = pl.program_id(1)
    head = lambda h: slice(h * FOX_DIM, (h + 1) * FOX_DIM)

    @pl.when(j == 0)
    def _():
        for h in range(FOX_HEADS):
            qh_ref[h] = q_ref[0, :, head(h)]
        _init_stats(m_ref, l_ref, acc_ref)

    def attend_all(scores, c_rows, mask, weighted_values):
        s_all = [scores(h) for h in range(FOX_HEADS)]
        stats = [_softmax_update(s_all[h], [cn_ref[0, h:h + 1, 0:1] - c_rows[0, h:h + 1, :]], mask,
                                 m_ref.at[h], l_ref.at[h]) for h in range(FOX_HEADS)]
        for h, (p, alpha) in enumerate(stats):
            acc_ref[h] = alpha[:, :FOX_DIM] * acc_ref[h] + weighted_values(h, p)

    attend_all(lambda h: _dot(qh_ref[h], kc_ref[head(h), :].astype(BF16)), cc_ref, None,
               lambda h, p: _dot_nt(p, vc_ref[head(h), :].astype(BF16)))

    @pl.when(j == nblk - 1)
    def _():
        rr = lax.broadcasted_iota(jnp.int32, (t, t), 0)
        cc = lax.broadcasted_iota(jnp.int32, (t, t), 1)
        attend_all(lambda h: _dot_nt(qh_ref[h], kn_ref[0, :, head(h)]), cn_ref, cc <= rr,
                   lambda h, p: _dot(p, vn_ref[0, :, head(h)]))
        for h in range(FOX_HEADS):
            o = acc_ref[h] / jnp.sum(l_ref[h], axis=-1, keepdims=True)
            o_ref[0, :, head(h)] = o.astype(BF16)


def _fox_attn_cached(q, k_new, v_new, c_new, k_cache, v_cache, c_cache):
    b, t, _ = q.shape
    plen = c_cache.shape[-1]
    tkc = min(2048, plen)
    nblk = plen // tkc
    new = pl.BlockSpec((1, t, FOX_W), lambda i, j: (i, 0, 0))
    cache = pl.BlockSpec((FOX_W, tkc), lambda i, j: (i, j))
    return pl.pallas_call(
        functools.partial(_fox_cached_kernel, t=t, nblk=nblk, tkc=tkc),
        grid=(b, nblk),
        in_specs=[new, new, new, pl.BlockSpec((1, FOX_HEADS, t), lambda i, j: (i, 0, 0)),
                  cache, cache, pl.BlockSpec((1, FOX_HEADS, tkc), lambda i, j: (i, 0, j))],
        out_specs=new,
        out_shape=jax.ShapeDtypeStruct((b, t, FOX_W), BF16),
        scratch_shapes=[pltpu.VMEM((FOX_HEADS, t, FOX_DIM), BF16), pltpu.VMEM((FOX_HEADS, t, LANES), F32),
                        pltpu.VMEM((FOX_HEADS, t, LANES), F32), pltpu.VMEM((FOX_HEADS, t, FOX_DIM), F32)],
        compiler_params=_params("parallel", "arbitrary"),
        name="fox_attn_cached",
    )(q, k_new, v_new, c_new, k_cache, v_cache, c_cache)


_XP_HEAD = SUBLANES


def _mlstm_kernel(mqk_ref, mv_ref, mo_ref, sm_ref, smt_ref, cw_ref, cb_ref, hn_ref,
                  conv0_ref, c0_ref, n0_ref, m0_ref,
                  o_ref, conv_out_ref, c_out_ref, n_out_ref, m_out_ref,
                  xp_ref, c_ref, n_ref, m_ref, *, L):
    ci = pl.program_id(1)
    prev = ML_CONV - 1

    @pl.when(ci == 0)
    def _():
        xp_ref[_XP_HEAD - prev:_XP_HEAD, :] = conv0_ref[0]
        c_ref[...] = c0_ref[0]
        n_ref[...] = n0_ref[0]
        m_ref[...] = m0_ref[0]

    xp_ref[_XP_HEAD:_XP_HEAD + L, :] = mqk_ref[...]
    y = cb_ref[...] + sum(xp_ref[_XP_HEAD - prev + j:_XP_HEAD - prev + j + L, :] * cw_ref[j:j + 1, :]
                          for j in range(ML_CONV))
    qk = y * _sigmoid(y)
    tail = xp_ref[_XP_HEAD + L - prev:_XP_HEAD + L, :]
    conv_out_ref[0] = tail
    xp_ref[_XP_HEAD - prev:_XP_HEAD, :] = tail

    rr = lax.broadcasted_iota(jnp.int32, (L, L), 0)
    cc = lax.broadcasted_iota(jnp.int32, (L, L), 1)
    causal = cc <= rr
    sm = sm_ref[...]
    smt = smt_ref[0]
    hi = lax.Precision.HIGHEST
    bt_cols = jnp.dot(causal.astype(F32), sm, precision=hi, preferred_element_type=F32)
    bt_rows = jnp.dot(smt, (rr <= cc).astype(F32), precision=hi, preferred_element_type=F32)

    heads = range(ML_HEADS)
    sl = [slice(h * ML_DIM, (h + 1) * ML_DIM) for h in heads]
    q = [qk[:, sl[h]] * (ML_DIM ** -0.5) for h in heads]
    k = [qk[:, ML_W + h * ML_DIM:ML_W + (h + 1) * ML_DIM] for h in heads]
    qb = [a.astype(BF16) for a in q]
    kb = [a.astype(BF16) for a in k]
    c_p = [c_ref[h] for h in heads]
    n_p = [n_ref[h] for h in heads]
    m_p = [m_ref[h][:, 0:1] for h in heads]
    qk_scores = [_dot_nt(qb[h], kb[h]) for h in heads]
    q_c = [_dot_nt(qb[h], c_p[h].astype(BF16)) for h in heads]

    s_list, w_inter, den, m_t, wg, g_max, b_end = [], [], [], [], [], [], []
    for h in heads:
        it_col = sm[:, _SM_MI + h:_SM_MI + h + 1]
        it_row = smt[_SM_MI + h:_SM_MI + h + 1, :]
        bt_col = bt_cols[:, _SM_MF + h:_SM_MF + h + 1]
        bt_row = bt_rows[_SM_MF + h:_SM_MF + h + 1, :]
        b_end.append(bt_col[L - 1:L, :])
        log_w = jnp.where(causal, bt_col - bt_row + it_row, NEG_BIG)
        m_intra = jnp.max(log_w, axis=-1, keepdims=True)
        log_inter = bt_col + m_p[h]
        m_t.append(jnp.maximum(log_inter, m_intra))
        dmat = jnp.exp(log_w - m_t[h])
        s_list.append(qk_scores[h] * dmat)
        w_inter.append(jnp.exp(log_inter - m_t[h]))
        den.append(w_inter[h] * jnp.sum(q[h] * n_p[h], axis=-1, keepdims=True)
                   + jnp.sum(s_list[h], axis=-1, keepdims=True))
        g_col = b_end[h] - bt_col + it_col
        g_max.append(jnp.max(g_col, axis=0, keepdims=True))
        wg.append(jnp.exp(g_col - g_max[h]))

    s_v = [_dot(s_list[h].astype(BF16), mv_ref[:, sl[h]]) for h in heads]
    kv_blk = [_dot_tn((mv_ref[:, sl[h]].astype(F32) * wg[h]).astype(BF16), kb[h]) for h in heads]

    for h in heads:
        num = w_inter[h] * q_c[h] + s_v[h]
        hout = num / jnp.maximum(jnp.abs(den[h]), jnp.exp(-m_t[h]))
        hg = hout * _sigmoid(mo_ref[:, sl[h]])
        o_ref[:, sl[h]] = _rms(hg, hn_ref[:, sl[h]]).astype(BF16)

        k_blk = jnp.sum(k[h] * wg[h], axis=0, keepdims=True)
        m_new = jnp.maximum(b_end[h] + m_p[h], g_max[h])
        decay = jnp.exp(b_end[h] + m_p[h] - m_new)
        scale = jnp.exp(g_max[h] - m_new)
        c_new = decay * c_p[h] + scale * kv_blk[h]
        n_new = decay * n_p[h] + scale * k_blk
        c_ref[h] = c_new
        n_ref[h] = n_new
        m_ref[h] = jnp.broadcast_to(m_new, (1, LANES))
        c_out_ref[0, h] = c_new
        n_out_ref[0, h] = n_new
        m_out_ref[0, h] = jnp.broadcast_to(m_new, (1, LANES))


def _mlstm(mqk, mv, mo, sm, smt, W, conv0, c0, n0, m0, b, t):
    L = min(256, t)
    nc = t // L
    row = lambda w: pl.BlockSpec((L, w), lambda i, j: (i * nc + j, 0))
    st = lambda *shape: pl.BlockSpec((1,) + shape, lambda i, j: (i,) + (0,) * len(shape))
    return pl.pallas_call(
        functools.partial(_mlstm_kernel, L=L),
        grid=(b, nc),
        in_specs=[row(2 * ML_W), row(ML_W), row(ML_W), row(LANES),
                  pl.BlockSpec((1, 16, L), lambda i, j: (i, 0, j)),
                  _resident((ML_CONV, 2 * ML_W)), _resident((1, 2 * ML_W)), _resident((1, ML_W)),
                  st(ML_CONV - 1, 2 * ML_W), st(ML_HEADS, ML_DIM, ML_DIM),
                  st(ML_HEADS, 1, ML_DIM), st(ML_HEADS, 1, LANES)],
        out_specs=[row(ML_W), st(ML_CONV - 1, 2 * ML_W), st(ML_HEADS, ML_DIM, ML_DIM),
                   st(ML_HEADS, 1, ML_DIM), st(ML_HEADS, 1, LANES)],
        out_shape=[jax.ShapeDtypeStruct((b * t, ML_W), BF16),
                   jax.ShapeDtypeStruct((b, ML_CONV - 1, 2 * ML_W), F32),
                   jax.ShapeDtypeStruct((b, ML_HEADS, ML_DIM, ML_DIM), F32),
                   jax.ShapeDtypeStruct((b, ML_HEADS, 1, ML_DIM), F32),
                   jax.ShapeDtypeStruct((b, ML_HEADS, 1, LANES), F32)],
        scratch_shapes=[pltpu.VMEM((_XP_HEAD + L, 2 * ML_W), F32),
                        pltpu.VMEM((ML_HEADS, ML_DIM, ML_DIM), F32),
                        pltpu.VMEM((ML_HEADS, 1, ML_DIM), F32),
                        pltpu.VMEM((ML_HEADS, 1, LANES), F32)],
        compiler_params=_params("parallel", "arbitrary"),
        name="mlstm",
    )(mqk, mv, mo, sm, smt, W["mlstm_conv_w"], W["mlstm_conv_b"], W["mlstm_head_norm"],
      conv0, c0, n0, m0)


def _mem_attn_kernel(q_ref, k_ref, v_ref, o_ref):
    sl = [slice(h * MEM_DIM, (h + 1) * MEM_DIM) for h in range(MEM_HEADS)]
    mlen = k_ref.shape[0] // MEM_HEADS
    head = lambda ref, h: ref[pl.ds(h, mlen, stride=MEM_HEADS), :].astype(BF16)
    s = [_dot_nt(q_ref[:, c], head(k_ref, h)) for h, c in enumerate(sl)]
    p = [jnp.exp(a - jnp.max(a, axis=-1, keepdims=True)) for a in s]
    pv = [_dot(a.astype(BF16), head(v_ref, h)) for h, a in enumerate(p)]
    for a, o, c in zip(p, pv, sl):
        o_ref[:, c] = (o / jnp.sum(a, axis=-1, keepdims=True)).astype(BF16)


def _mem_attn(cq, mem_k, mem_v, b, t):
    tm = min(512, t)
    nt = t // tm
    mlen = mem_k.shape[0] // (b * MEM_HEADS)
    kv = pl.BlockSpec((mlen * MEM_HEADS, MEM_DIM), lambda i: (i // nt, 0))
    return pl.pallas_call(
        _mem_attn_kernel,
        grid=(b * nt,),
        in_specs=[pl.BlockSpec((tm, MEM_W), lambda i: (i, 0)), kv, kv],
        out_specs=pl.BlockSpec((tm, MEM_W), lambda i: (i, 0)),
        out_shape=jax.ShapeDtypeStruct((b * t, MEM_W), BF16),
        compiler_params=_params("parallel"),
        name="mem_attn",
    )(cq, mem_k, mem_v)


def _merge_kernel(x_ref, a_ref, b_ref, m_ref, gpre_ref, wg_ref, bg_ref, wa_ref, wb_ref, wm_ref,
                  wo_ref, gpost_ref, o_ref):
    x = x_ref[...]
    xn = _rms(x, gpre_ref[...]).astype(BF16)
    merged = None
    for i, (br_ref, w_ref) in enumerate(((a_ref, wa_ref), (b_ref, wb_ref), (m_ref, wm_ref))):
        sl = slice(i * D_MODEL, (i + 1) * D_MODEL)
        gate = _sigmoid(_dot(xn, wg_ref[:, sl]) + bg_ref[:, sl])
        term = gate * _dot(br_ref[...], w_ref[...])
        merged = term if merged is None else merged + term
    o_ref[...] = x + _rms(_dot(merged.astype(BF16), wo_ref[...]), gpost_ref[...])


def _merge(x2, a, b, m, W):
    n = x2.shape[0]
    tm = min(512, n)
    row = lambda w: pl.BlockSpec((tm, w), lambda i: (i, 0))
    return pl.pallas_call(
        _merge_kernel,
        grid=(n // tm,),
        in_specs=[row(D_MODEL), row(FOX_W), row(ML_W), row(MEM_W), _resident((1, D_MODEL)),
                  _resident((D_MODEL, N_BRANCH * D_MODEL)), _resident((1, N_BRANCH * D_MODEL)),
                  _resident((FOX_W, D_MODEL)), _resident((ML_W, D_MODEL)), _resident((MEM_W, D_MODEL)),
                  _resident((D_MODEL, D_MODEL)), _resident((1, D_MODEL))],
        out_specs=row(D_MODEL),
        out_shape=jax.ShapeDtypeStruct((n, D_MODEL), F32),
        compiler_params=_params("parallel"),
        name="merge",
    )(x2, a, b, m, W["norm_mix_pre"], W["w_gate"], W["b_gate"], W["w_br_a"], W["w_br_b"],
      W["w_br_m"], W["w_out"], W["norm_mix_post"])


_FF_CHUNK = 512


def _ffn_kernel(x_ref, gpre_ref, wup_ref, cw_ref, cb_ref, wdn_ref, gpost_ref, conv0_ref,
                o_ref, conv_out_ref, buf_ref, carry_ref, hid_ref, *, tm):
    ti = pl.program_id(1)
    prev = FFN_CONV - 1

    @pl.when(ti == 0)
    def _():
        carry_ref[SUBLANES - prev:SUBLANES, :] = conv0_ref[0]

    x = x_ref[...]
    xn = _rms(x, gpre_ref[...]).astype(BF16)

    def conv_cols(lo, w):
        up = _dot(xn, wup_ref[:, lo:lo + w])
        buf_ref[SUBLANES - prev:SUBLANES, 0:w] = carry_ref[SUBLANES - prev:SUBLANES, lo:lo + w]
        buf_ref[SUBLANES:SUBLANES + tm, 0:w] = up
        tail = up[tm - prev:tm, :]
        carry_ref[SUBLANES - prev:SUBLANES, lo:lo + w] = tail
        conv_out_ref[0, :, lo:lo + w] = tail
        return cb_ref[:, lo:lo + w] + sum(
            buf_ref[SUBLANES - prev + j:SUBLANES - prev + j + tm, 0:w] * cw_ref[j:j + 1, lo:lo + w]
            for j in range(FFN_CONV))

    for lo in range(0, D_FF, _FF_CHUNK):
        w = min(_FF_CHUNK, D_FF - lo)
        ua = conv_cols(lo, w)
        ub = conv_cols(D_FF + lo, w)
        hid_ref[:, lo:lo + w] = (_gelu_tanh(ua) * ub).astype(BF16)
    o_ref[...] = x + _rms(_dot(hid_ref[...], wdn_ref[...]), gpost_ref[...])


def _ffn_short_kernel(x_ref, gpre_ref, wup_ref, cw_ref, cb_ref, wdn_ref, gpost_ref, conv0_ref,
                      o_ref, conv_out_ref, buf_ref, ybuf_ref, hid_ref, *, t, nseq):
    prev = FFN_CONV - 1
    tm = t * nseq
    x = x_ref[...]
    xn = _rms(x, gpre_ref[...]).astype(BF16)
    buf_ref[0:SUBLANES, :] = jnp.zeros((SUBLANES, buf_ref.shape[1]), F32)

    def conv_cols(lo, w):
        cols = slice(lo, lo + w)
        up = _dot(xn, wup_ref[:, cols])
        buf_ref[SUBLANES:SUBLANES + tm, 0:w] = up
        taps = [cw_ref[j:j + 1, cols] for j in range(FFN_CONV)]
        ybuf_ref[:, 0:w] = cb_ref[:, cols] + sum(
            buf_ref[SUBLANES - prev + j:SUBLANES - prev + j + tm, 0:w] * taps[j] for j in range(FFN_CONV))
        for s in range(nseq):
            r0 = s * t
            u0, u1 = up[r0:r0 + 1, :], up[r0 + 1:r0 + 2, :]
            p0, p1 = conv0_ref[s, 0:1, cols], conv0_ref[s, 1:2, cols]
            y0 = cb_ref[:, cols] + p0 * taps[0] + p1 * taps[1] + u0 * taps[2]
            y1 = cb_ref[:, cols] + p1 * taps[0] + u0 * taps[1] + u1 * taps[2]
            ybuf_ref[r0:r0 + prev, 0:w] = jnp.concatenate([y0, y1], axis=0)
            conv_out_ref[s, :, cols] = up[r0 + t - prev:r0 + t, :]
        return ybuf_ref[:, 0:w]

    for lo in range(0, D_FF, _FF_CHUNK):
        w = min(_FF_CHUNK, D_FF - lo)
        ua = conv_cols(lo, w)
        ub = conv_cols(D_FF + lo, w)
        hid_ref[:, lo:lo + w] = (_gelu_tanh(ua) * ub).astype(BF16)
    o_ref[...] = x + _rms(_dot(hid_ref[...], wdn_ref[...]), gpost_ref[...])


def _ffn_short(x1, conv0, W, b, t):
    nseq = min(b, 512 // t)
    tm = nseq * t
    row = pl.BlockSpec((tm, D_MODEL), lambda i: (i, 0))
    st = pl.BlockSpec((nseq, FFN_CONV - 1, 2 * D_FF), lambda i: (i, 0, 0))
    return pl.pallas_call(
        functools.partial(_ffn_short_kernel, t=t, nseq=nseq),
        grid=(b // nseq,),
        in_specs=[row, _resident((1, D_MODEL)), _resident((D_MODEL, 2 * D_FF)),
                  _resident((FFN_CONV, 2 * D_FF)), _resident((1, 2 * D_FF)),
                  _resident((D_FF, D_MODEL)), _resident((1, D_MODEL)), st],
        out_specs=[row, st],
        out_shape=[jax.ShapeDtypeStruct((b * t, D_MODEL), F32),
                   jax.ShapeDtypeStruct((b, FFN_CONV - 1, 2 * D_FF), F32)],
        scratch_shapes=[pltpu.VMEM((SUBLANES + tm, _FF_CHUNK), F32), pltpu.VMEM((tm, _FF_CHUNK), F32),
                        pltpu.VMEM((tm, D_FF), BF16)],
        compiler_params=_params("parallel"),
        name="ffn_short",
    )(x1, W["norm_ffn_pre"], W["w_up"], W["ffn_conv_w"], W["ffn_conv_b"], W["w_down"],
      W["norm_ffn_post"], conv0)


def _ffn(x1, conv0, W, b, t):
    if 2 * t <= 512 and 512 % t == 0 and b % min(b, 512 // t) == 0 and t >= SUBLANES:
        return _ffn_short(x1, conv0, W, b, t)
    tm = min(512, t)
    nt = t // tm
    row = pl.BlockSpec((tm, D_MODEL), lambda i, j: (i * nt + j, 0))
    st = pl.BlockSpec((1, FFN_CONV - 1, 2 * D_FF), lambda i, j: (i, 0, 0))
    return pl.pallas_call(
        functools.partial(_ffn_kernel, tm=tm),
        grid=(b, nt),
        in_specs=[row, _resident((1, D_MODEL)), _resident((D_MODEL, 2 * D_FF)),
                  _resident((FFN_CONV, 2 * D_FF)), _resident((1, 2 * D_FF)),
                  _resident((D_FF, D_MODEL)), _resident((1, D_MODEL)), st],
        out_specs=[row, st],
        out_shape=[jax.ShapeDtypeStruct((b * t, D_MODEL), F32),
                   jax.ShapeDtypeStruct((b, FFN_CONV - 1, 2 * D_FF), F32)],
        scratch_shapes=[pltpu.VMEM((SUBLANES + tm, _FF_CHUNK), F32),
                        pltpu.VMEM((SUBLANES, 2 * D_FF), F32), pltpu.VMEM((tm, D_FF), BF16)],
        compiler_params=_params("parallel", "arbitrary"),
        name="ffn",
    )(x1, W["norm_ffn_pre"], W["w_up"], W["ffn_conv_w"], W["ffn_conv_b"], W["w_down"],
      W["norm_ffn_post"], conv0)


def _prep_weights(norm_mix_pre, w_in, b_in, fox_q_norm, fox_k_norm, mlstm_conv_w, mlstm_conv_b,
                  mlstm_head_norm, w_br_a, w_br_b, w_br_m, w_out, norm_mix_post, norm_ffn_pre,
                  w_up, ffn_conv_w, ffn_conv_b, w_down, norm_ffn_post):
    o_ff, o_mq, o_mi, o_mf, o_mo, o_cq, o_g = 1536, 1544, 3080, 3084, 3088, 3600, 4112

    def pack(a):
        pad = jnp.zeros(a.shape[:-1] + (LANES - FOX_HEADS - 2 * ML_HEADS,), a.dtype)
        small = jnp.concatenate([a[..., o_ff:o_mq], a[..., o_mi:o_mf], a[..., o_mf:o_mo], pad], -1)
        return jnp.concatenate([a[..., :o_ff], a[..., o_mq:o_mi], a[..., o_mo:o_g], small], -1)

    row = lambda v: v.reshape(1, -1).astype(F32)
    head = jnp.arange(FOX_W) // FOX_DIM
    return {
        "norm_mix_pre": row(norm_mix_pre),
        "w_in_a": w_in[:, :o_ff].astype(BF16),
        "w_in_b": w_in[:, o_mq:o_mi].astype(BF16),
        "w_in_c": w_in[:, o_mo:o_g].astype(BF16),
        "w_in_s": jnp.concatenate([w_in[:, o_ff:o_mq], w_in[:, o_mi:o_mo],
                                   jnp.zeros((D_MODEL, LANES - FOX_HEADS - 2 * ML_HEADS), F32)], -1).astype(BF16),
        "b_in": row(pack(b_in)),
        "w_gate": w_in[:, o_g:].astype(BF16),
        "b_gate": row(b_in[o_g:]),
        "fox_q_norm": row(jnp.tile(fox_q_norm, FOX_HEADS)),
        "fox_k_norm": row(jnp.tile(fox_k_norm, FOX_HEADS)),
        "head_avg": ((head[:, None] == head[None, :]).astype(F32) / FOX_DIM).astype(BF16),
        "mlstm_conv_w": mlstm_conv_w.astype(F32),
        "mlstm_conv_b": row(mlstm_conv_b),
        "mlstm_head_norm": row(mlstm_head_norm),
        "w_br_a": w_br_a.astype(BF16), "w_br_b": w_br_b.astype(BF16), "w_br_m": w_br_m.astype(BF16),
        "w_out": w_out.astype(BF16),
        "norm_mix_post": row(norm_mix_post),
        "norm_ffn_pre": row(norm_ffn_pre),
        "w_up": w_up.astype(BF16),
        "ffn_conv_w": ffn_conv_w.astype(F32),
        "ffn_conv_b": row(ffn_conv_b),
        "w_down": w_down.astype(BF16),
        "norm_ffn_post": row(norm_ffn_post),
    }


def _rows_layout(a, b, t, r):
    return jnp.transpose(a[:, :r].reshape(b, t, r), (0, 2, 1))


def _layer(x, fox_cache, ml_c0, ml_n0, ml_m0, ml_conv_prev, mem_k, mem_v, ffn_conv_prev, W):
    b, t, d = x.shape
    n = b * t
    x2 = x.reshape(n, d)
    q, kb, vb, fk, fv, mqk, mv, mo, cq, sm, smt = _in_proj(x2, W, b, t)
    if smt is None:
        smt = _rows_layout(sm, b, t, _SM_ROWS)
        f_log = sm[:, :FOX_HEADS].reshape(b, t, FOX_HEADS)
    else:
        fk, fv = (jnp.transpose(a.reshape(b, FOX_HEADS, FOX_DIM, t), (0, 3, 1, 2)) for a in (fk, fv))
        f_log = jnp.transpose(smt[:, :FOX_HEADS, :], (0, 2, 1))

    pairs = FOX_HEADS // 2
    f_rows = smt[:, :FOX_HEADS, :].reshape(b * FOX_HEADS, t)
    zeros_init = jnp.zeros((b * FOX_HEADS, LANES), F32)
    q3, fk3, fv3 = q.reshape(b, t, FOX_W), kb.reshape(b, t, FOX_W), vb.reshape(b, t, FOX_W)
    if fox_cache is None:
        c_new = _cumsum_rows(f_rows, zeros_init)
        a_out = _fox_attn_prompt(q3, fk3, fv3, c_new.reshape(b, pairs, 2, t))
    else:
        k_cache, v_cache, logf_cache = fox_cache
        plen = k_cache.shape[1]
        logf_rows = jnp.transpose(logf_cache.astype(F32), (0, 2, 1)).reshape(b * FOX_HEADS, plen)
        c_cache = _cumsum_rows(logf_rows, zeros_init)
        c_new = _cumsum_rows(f_rows, jnp.broadcast_to(c_cache[:, plen - 1:plen], (b * FOX_HEADS, LANES)))
        feature_major = lambda a: jnp.transpose(a.astype(F32), (0, 2, 3, 1)).reshape(b * FOX_W, plen)
        a_out = _fox_attn_cached(q3, fk3, fv3, c_new.reshape(b, FOX_HEADS, t), feature_major(k_cache),
                                 feature_major(v_cache), c_cache.reshape(b, FOX_HEADS, plen))

    b_out, ml_conv_new, c_t, n_t, m_t = _mlstm(
        mqk, mv, mo, sm, smt, W, ml_conv_prev.astype(F32), ml_c0.astype(F32),
        ml_n0.astype(F32).reshape(b, ML_HEADS, 1, ML_DIM),
        jnp.broadcast_to(ml_m0.astype(F32)[:, :, None, None], (b, ML_HEADS, 1, LANES)), b, t)

    m_out = _mem_attn(cq, mem_k.astype(F32).reshape(-1, MEM_DIM), mem_v.astype(F32).reshape(-1, MEM_DIM), b, t)

    x1 = _merge(x2, a_out.reshape(n, FOX_W), b_out, m_out, W)
    y, ffn_conv_new = _ffn(x1, ffn_conv_prev.astype(F32), W, b, t)

    states = (fk.reshape(b, t, FOX_HEADS, FOX_DIM), fv.reshape(b, t, FOX_HEADS, FOX_DIM),
              f_log, c_t, n_t.reshape(b, ML_HEADS, ML_DIM),
              m_t[:, :, 0, 0], ml_conv_new, ffn_conv_new)
    return y.reshape(b, t, d), states


def kernel(x_prompt, x_sample, cache_fox_k, cache_fox_v, cache_fox_logf, state_mlstm_c, state_mlstm_n, state_mlstm_m, state_mlstm_conv, cache_mem_k, cache_mem_v, state_ffn_conv, mem_prompt, norm_mix_pre, w_in, b_in, fox_q_norm, fox_k_norm, mlstm_conv_w, mlstm_conv_b, mlstm_head_norm, norm_mem, w_mem_kv, w_br_a, w_br_b, w_br_m, w_out, norm_mix_post, norm_ffn_pre, w_up, ffn_conv_w, ffn_conv_b, w_down, norm_ffn_post):
    depth = w_in.shape[0]
    hp, hs = x_prompt, x_sample
    b = x_prompt.shape[0]
    new_p = [[] for _ in range(10)]
    new_s = [[] for _ in range(8)]
    for l in range(depth):
        W = _prep_weights(norm_mix_pre[l], w_in[l], b_in[l], fox_q_norm[l], fox_k_norm[l],
                          mlstm_conv_w[l], mlstm_conv_b[l], mlstm_head_norm[l], w_br_a[l], w_br_b[l],
                          w_br_m[l], w_out[l], norm_mix_post[l], norm_ffn_pre[l], w_up[l],
                          ffn_conv_w[l], ffn_conv_b[l], w_down[l], norm_ffn_post[l])
        mlen = mem_prompt.shape[1]
        mem_k_p, mem_v_p = (a.reshape(b, mlen, MEM_HEADS, MEM_DIM) for a in _mem_kv(
            mem_prompt.reshape(b * mlen, D_MODEL), norm_mem[l].reshape(1, -1).astype(F32),
            w_mem_kv[l].astype(BF16)))
        hp, st_p = _layer(
            hp, None,
            jnp.zeros((b, ML_HEADS, ML_DIM, ML_DIM), F32), jnp.zeros((b, ML_HEADS, ML_DIM), F32),
            jnp.zeros((b, ML_HEADS), F32), jnp.zeros((b, ML_CONV - 1, 2 * ML_W), F32),
            mem_k_p, mem_v_p, jnp.zeros((b, FFN_CONV - 1, 2 * D_FF), F32), W)
        hs, st_s = _layer(
            hs, (cache_fox_k[l], cache_fox_v[l], cache_fox_logf[l]),
            state_mlstm_c[l], state_mlstm_n[l], state_mlstm_m[l], state_mlstm_conv[l],
            cache_mem_k[l], cache_mem_v[l], state_ffn_conv[l], W)
        for acc, a in zip(new_p, st_p + (mem_k_p, mem_v_p)):
            acc.append(a)
        for acc, a in zip(new_s, st_s):
            acc.append(a)
    sp = [jnp.stack(a, axis=0) for a in new_p]
    ss = [jnp.stack(a, axis=0) for a in new_s]
    return (hp, hs, sp[0], sp[1], sp[2], sp[3], sp[4], sp[5], sp[6], sp[7], sp[8], sp[9],
            ss[0], ss[1], ss[2], ss[3], ss[4], ss[5], ss[6], ss[7])
```
